```python
import math
import jax, jax.numpy as jnp
from jax import lax
import numpy as np

D_MODEL = 1024
BATCH = 16
SEQ = 4096
DEPTH = 4

N_MIXERS = 3
BLOCK = 128
EPS = 1e-6
SB_HEADS = 16
SB_HEAD_DIM = D_MODEL // SB_HEADS
DIL_GROUPS = ((128, 1), (512, 4), (2048, 16))
DIL_HEADS = 8
DIL_HEAD_DIM = D_MODEL // DIL_HEADS
MLA_HEADS = 16
MLA_NOPE = 64
MLA_ROPE = 32
MLA_V = 64
MLA_Q_RANK = 384
MLA_KV_RANK = 256
ROPE_THETA = 10000.0
D_FF = 4 * D_MODEL

kernel_name = "hybrid_sb_dilated_mla_trunk"


def rmsnorm(x, g):
    xf = x.astype(jnp.float32)
    y = xf * lax.rsqrt(jnp.mean(xf * xf, axis=-1, keepdims=True) + EPS)
    return (y * g.astype(jnp.float32)).astype(x.dtype)


def alibi_slopes(n):
    return jnp.exp2(-8.0 * jnp.arange(1, n + 1, dtype=jnp.float32) / n)


def apply_rope(x, pos):
    half = x.shape[-1] // 2
    inv = ROPE_THETA ** (-jnp.arange(half, dtype=jnp.float32) / half)
    ang = pos.astype(jnp.float32)[:, None] * inv[None, :]
    cos = jnp.cos(ang)[None, :, None, :]
    sin = jnp.sin(ang)[None, :, None, :]
    x1 = x[..., :half].astype(jnp.float32)
    x2 = x[..., half:].astype(jnp.float32)
    return jnp.concatenate([x1 * cos - x2 * sin, x2 * cos + x1 * sin], axis=-1).astype(x.dtype)


def stick_breaking_mixer(h, w_in, q_gain, k_gain, w_out):
    B, S, _ = h.shape
    q, k, v = jnp.split(h @ w_in, 3, axis=-1)
    q = rmsnorm(q.reshape(B, S, SB_HEADS, SB_HEAD_DIM), q_gain).transpose(0, 2, 1, 3)
    k = rmsnorm(k.reshape(B, S, SB_HEADS, SB_HEAD_DIM), k_gain).transpose(0, 2, 1, 3)
    v = v.reshape(B, S, SB_HEADS, SB_HEAD_DIM).transpose(0, 2, 1, 3)
    scale = 1.0 / math.sqrt(SB_HEAD_DIM)
    outs = []
    for n in range(S // BLOCK):
        q0 = n * BLOCK
        end = q0 + BLOCK
        z = jnp.einsum('bhqe,bhke->bhqk', q[:, :, q0:end], k[:, :, :end],
                       preferred_element_type=jnp.float32) * scale
        past = jnp.arange(end)[None, :] < (q0 + jnp.arange(BLOCK))[:, None]
        log_keep = jnp.where(past, jax.nn.log_sigmoid(-z), 0.0)
        suffix = lax.cumsum(log_keep, axis=3, reverse=True) - log_keep
        a = jnp.where(past, jnp.exp(jax.nn.log_sigmoid(z) + suffix), 0.0)
        outs.append(jnp.einsum('bhqk,bhke->bhqe', a, v[:, :, :end].astype(jnp.float32)))
    o = jnp.concatenate(outs, axis=2).astype(h.dtype)
    return o.transpose(0, 2, 1, 3).reshape(B, S, SB_HEADS * SB_HEAD_DIM) @ w_out


def dilated_branch(q, k, v, dilation, max_steps, slopes):
    B, S, H, E = q.shape
    L = S // dilation
    nb = -(-L // BLOCK)
    Lp = nb * BLOCK

    def strided(t):
        t = t.reshape(B, L, dilation, H, E).transpose(0, 2, 3, 1, 4)
        return jnp.pad(t, ((0, 0), (0, 0), (0, 0), (0, Lp - L), (0, 0)))

    def banded(t):
        t = jnp.pad(t, ((0, 0), (0, 0), (0, 0), (BLOCK, 0), (0, 0)))
        t = t.reshape(B, dilation, H, nb + 1, BLOCK, E)
        return jnp.concatenate([t[:, :, :, :-1], t[:, :, :, 1:]], axis=4)

    qb = strided(q).reshape(B, dilation, H, nb, BLOCK, E)
    kb = banded(strided(k))
    vb = banded(strided(v))
    z = jnp.einsum('bdhnqe,bdhnke->bdhnqk', qb, kb,
                   preferred_element_type=jnp.float32) * (E ** -0.5)
    steps = BLOCK + jnp.arange(BLOCK)[:, None] - jnp.arange(2 * BLOCK)[None, :]
    key_pos = (jnp.arange(nb)[:, None, None] * BLOCK
               + jnp.arange(2 * BLOCK)[None, None, :] - BLOCK)
    valid = (steps >= 0) & (steps <= max_steps) & (key_pos >= 0)
    z = z - slopes[:, None, None, None] * (dilation * steps).astype(jnp.float32)
    z = jnp.where(valid, z, -jnp.inf)
    m = jnp.max(z, axis=-1, keepdims=True)
    p = jnp.exp(z - m)
    denom = jnp.sum(p, axis=-1)
    o = jnp.einsum('bdhnqk,bdhnke->bdhnqe', p, vb.astype(jnp.float32)) / denom[..., None]
    lse = m[..., 0] + jnp.log(denom)
    o = o.reshape(B, dilation, H, Lp, E)[:, :, :, :L].transpose(0, 3, 1, 2, 4).reshape(B, S, H, E)
    lse = lse.reshape(B, dilation, H, Lp)[:, :, :, :L].transpose(0, 3, 1, 2).reshape(B, S, H)
    return o, lse


def dilated_mixer(h, w_in, q_gain, k_gain, w_out):
    B, S, _ = h.shape
    G = len(DIL_GROUPS)
    qkv = (h @ w_in).reshape(B, S, G, 3, DIL_HEADS, DIL_HEAD_DIM)
    slopes = alibi_slopes(G * DIL_HEADS).reshape(G, DIL_HEADS)
    outs, lses = [], []
    for g, (window, dilation) in enumerate(DIL_GROUPS):
        q = rmsnorm(qkv[:, :, g, 0], q_gain[g])
        k = rmsnorm(qkv[:, :, g, 1], k_gain[g])
        o, lse = dilated_branch(q, k, qkv[:, :, g, 2], dilation, window // dilation, slopes[g])
        outs.append(o)
        lses.append(lse)
    alpha = jax.nn.softmax(jnp.stack(lses, axis=-1), axis=-1)
    o = jnp.einsum('bshg,gbshe->bshe', alpha, jnp.stack(outs, axis=0))
    return o.reshape(B, S, DIL_HEADS * DIL_HEAD_DIM).astype(h.dtype) @ w_out


def causal_softmax_attention(q, k, v, scale):
    B, S, H, _ = q.shape
    qt, kt, vt = (t.transpose(0, 2, 1, 3) for t in (q, k, v))
    outs = []
    for n in range(S // BLOCK):
        q0 = n * BLOCK
        end = q0 + BLOCK
        z = jnp.einsum('bhqe,bhke->bhqk', qt[:, :, q0:end], kt[:, :, :end],
                       preferred_element_type=jnp.float32) * scale
        causal = jnp.arange(end)[None, :] <= (q0 + jnp.arange(BLOCK))[:, None]
        p = jax.nn.softmax(jnp.where(causal, z, -jnp.inf), axis=-1)
        outs.append(jnp.einsum('bhqk,bhkv->bhqv', p, vt[:, :, :end].astype(jnp.float32)))
    return jnp.concatenate(outs, axis=2).astype(v.dtype).transpose(0, 2, 1, 3)


def mla_mixer(h, w_in, q_a_gain, kv_a_gain, w_q_b, w_kv_b, q_gain, k_gain, w_out):
    B, S, _ = h.shape
    cq, ckv, k_rope = jnp.split(h @ w_in, [MLA_Q_RANK, MLA_Q_RANK + MLA_KV_RANK], axis=-1)
    q = (rmsnorm(cq, q_a_gain) @ w_q_b).reshape(B, S, MLA_HEADS, MLA_NOPE + MLA_ROPE)
    kv = (rmsnorm(ckv, kv_a_gain) @ w_kv_b).reshape(B, S, MLA_HEADS, MLA_NOPE + MLA_V)
    k_nope, v = kv[..., :MLA_NOPE], kv[..., MLA_NOPE:]
    k_rope = jnp.broadcast_to(k_rope[:, :, None, :], (B, S, MLA_HEADS, MLA_ROPE))
    k = jnp.concatenate([k_nope, k_rope], axis=-1)
    q = rmsnorm(q, q_gain)
    k = rmsnorm(k, k_gain)
    pos = jnp.arange(S)
    q = jnp.concatenate([q[..., :MLA_NOPE], apply_rope(q[..., MLA_NOPE:], pos)], axis=-1)
    k = jnp.concatenate([k[..., :MLA_NOPE], apply_rope(k[..., MLA_NOPE:], pos)], axis=-1)
    o = causal_softmax_attention(q, k, v, 1.0 / math.sqrt(MLA_NOPE + MLA_ROPE))
    return o.reshape(B, S, MLA_HEADS * MLA_V) @ w_out


def squared_relu_mlp(h, w_up, w_down):
    return jnp.square(jax.nn.relu(h @ w_up)) @ w_down


def _dense(key, fan_in, fan_out):
    return jax.random.normal(key, (fan_in, fan_out), jnp.float32) * (fan_in ** -0.5)


def _gain(key, shape):
    return 1.0 + 0.02 * jax.random.normal(key, shape, jnp.float32)


def setup_inputs(seed: int = 0) -> dict:
    key = jax.random.key(seed)
    kx, kp = jax.random.split(key)
    inputs = {"x": jax.random.normal(kx, (BATCH, SEQ, D_MODEL), jnp.float32)}
    layer_keys = jax.random.split(kp, DEPTH)
    G = len(DIL_GROUPS)
    for i in range(DEPTH):
        ks = jax.random.split(layer_keys[i], 12)
        p = "l%d_" % i
        inputs[p + "mix_norm"] = _gain(ks[0], (D_MODEL,))
        kind = i % N_MIXERS
        if kind == 0:
            inputs[p + "sb_w_in"] = _dense(ks[1], D_MODEL, 3 * SB_HEADS * SB_HEAD_DIM)
            inputs[p + "sb_q_norm"] = _gain(ks[2], (SB_HEAD_DIM,))
            inputs[p + "sb_k_norm"] = _gain(ks[3], (SB_HEAD_DIM,))
            inputs[p + "sb_w_out"] = _dense(ks[4], SB_HEADS * SB_HEAD_DIM, D_MODEL)
        elif kind == 1:
            inputs[p + "dil_w_in"] = _dense(ks[1], D_MODEL, G * 3 * DIL_HEADS * DIL_HEAD_DIM)
            inputs[p + "dil_q_norm"] = _gain(ks[2], (G, DIL_HEAD_DIM))
            inputs[p + "dil_k_norm"] = _gain(ks[3], (G, DIL_HEAD_DIM))
            inputs[p + "dil_w_out"] = _dense(ks[4], DIL_HEADS * DIL_HEAD_DIM, D_MODEL)
        else:
            inputs[p + "mla_w_in"] = _dense(ks[1], D_MODEL, MLA_Q_RANK + MLA_KV_RANK + MLA_ROPE)
            inputs[p + "mla_q_a_norm"] = _gain(ks[2], (MLA_Q_RANK,))
            inputs[p + "mla_kv_a_norm"] = _gain(ks[3], (MLA_KV_RANK,))
            inputs[p + "mla_w_q_b"] = _dense(ks[4], MLA_Q_RANK, MLA_HEADS * (MLA_NOPE + MLA_ROPE))
            inputs[p + "mla_w_kv_b"] = _dense(ks[5], MLA_KV_RANK, MLA_HEADS * (MLA_NOPE + MLA_V))
            inputs[p + "mla_q_norm"] = _gain(ks[6], (MLA_NOPE + MLA_ROPE,))
            inputs[p + "mla_k_norm"] = _gain(ks[7], (MLA_NOPE + MLA_ROPE,))
            inputs[p + "mla_w_out"] = _dense(ks[8], MLA_HEADS * MLA_V, D_MODEL)
        inputs[p + "mlp_norm"] = _gain(ks[9], (D_MODEL,))
        inputs[p + "mlp_w_up"] = _dense(ks[10], D_MODEL, D_FF)
        inputs[p + "mlp_w_down"] = _dense(ks[11], D_FF, D_MODEL)
    return inputs


def reference(x,
              l0_mix_norm, l0_sb_w_in, l0_sb_q_norm, l0_sb_k_norm, l0_sb_w_out,
              l0_mlp_norm, l0_mlp_w_up, l0_mlp_w_down,
              l1_mix_norm, l1_dil_w_in, l1_dil_q_norm, l1_dil_k_norm, l1_dil_w_out,
              l1_mlp_norm, l1_mlp_w_up, l1_mlp_w_down,
              l2_mix_norm, l2_mla_w_in, l2_mla_q_a_norm, l2_mla_kv_a_norm, l2_mla_w_q_b,
              l2_mla_w_kv_b, l2_mla_q_norm, l2_mla_k_norm, l2_mla_w_out,
              l2_mlp_norm, l2_mlp_w_up, l2_mlp_w_down,
              l3_mix_norm, l3_sb_w_in, l3_sb_q_norm, l3_sb_k_norm, l3_sb_w_out,
              l3_mlp_norm, l3_mlp_w_up, l3_mlp_w_down):
    layers = [
        (l0_mix_norm, (l0_sb_w_in, l0_sb_q_norm, l0_sb_k_norm, l0_sb_w_out),
         l0_mlp_norm, l0_mlp_w_up, l0_mlp_w_down),
        (l1_mix_norm, (l1_dil_w_in, l1_dil_q_norm, l1_dil_k_norm, l1_dil_w_out),
         l1_mlp_norm, l1_mlp_w_up, l1_mlp_w_down),
        (l2_mix_norm, (l2_mla_w_in, l2_mla_q_a_norm, l2_mla_kv_a_norm, l2_mla_w_q_b,
                       l2_mla_w_kv_b, l2_mla_q_norm, l2_mla_k_norm, l2_mla_w_out),
         l2_mlp_norm, l2_mlp_w_up, l2_mlp_w_down),
        (l3_mix_norm, (l3_sb_w_in, l3_sb_q_norm, l3_sb_k_norm, l3_sb_w_out),
         l3_mlp_norm, l3_mlp_w_up, l3_mlp_w_down),
    ]
    mixers = (stick_breaking_mixer, dilated_mixer, mla_mixer)
    h = x
    for i in range(DEPTH):
        mix_norm, mix_params, mlp_norm, w_up, w_down = layers[i]
        h = h + mixers[i % N_MIXERS](rmsnorm(h, mix_norm), *mix_params)
        h = h + squared_relu_mlp(rmsnorm(h, mlp_norm), w_up, w_down)
    return h
```

```python
import functools
import math

import jax
import jax.numpy as jnp
from jax import lax
from jax.experimental import pallas as pl
from jax.experimental.pallas import tpu as pltpu

F32 = jnp.float32
BF16 = jnp.bfloat16

EPS = 1e-6
LANES = 128
MASKED = -1e30
VMEM_LIMIT_BYTES = 48 * 1024 * 1024

D_MODEL = 1024
SB_HEADS = 16
SB_HEAD_DIM = 64
DIL_GROUPS = ((128, 1), (512, 4), (2048, 16))
DIL_HEADS = 8
DIL_HEAD_DIM = 128
DIL_BLOCK = 128
MLA_HEADS = 16
MLA_NOPE = 64
MLA_ROPE = 32
MLA_V = 64
MLA_QK = MLA_NOPE + MLA_ROPE
MLA_Q_RANK = 384
MLA_KV_RANK = 256
ROPE_THETA = 10000.0


def _params(*semantics):
    return pltpu.CompilerParams(dimension_semantics=semantics, vmem_limit_bytes=VMEM_LIMIT_BYTES)


def _rms(x, g):
    ms = jnp.mean(x * x, axis=-1, keepdims=True)
    return x * lax.rsqrt(ms + EPS) * g


def _split_dot(x, w):
    hi = x.astype(BF16)
    lo = (x - hi.astype(F32)).astype(BF16)
    return (jnp.dot(hi, w, preferred_element_type=F32)
            + jnp.dot(lo, w, preferred_element_type=F32))


def _dot_t(a, b):
    return lax.dot_general(a, b, (((1,), (1,)), ((), ())), preferred_element_type=F32)


def _proj_qkv_kernel(x_ref, g_ref, w_ref, gain_ref, seg_ref, o_ref, xn_ref, y_ref, *, head_dim):
    j = pl.program_id(1)

    @pl.when(j == 0)
    def _():
        xn_ref[...] = _rms(x_ref[...], g_ref[...]).astype(BF16)

    y_ref[...] = jnp.dot(xn_ref[...], w_ref[...], preferred_element_type=F32)

    @pl.when(j % 3 == 2)
    def _():
        o_ref[...] = y_ref[...].astype(BF16)

    @pl.when(j % 3 != 2)
    def _():
        for c in range(o_ref.shape[1] // LANES):
            sl = slice(c * LANES, (c + 1) * LANES)
            y = y_ref[:, sl]
            ms = _split_dot(y * y, seg_ref[...]) * (1.0 / head_dim)
            o_ref[:, sl] = (y * lax.rsqrt(ms + EPS) * gain_ref[:, sl]).astype(BF16)


def _proj_qkv(h, norm_g, w, gains, head_dim, tm=1024):
    t, d = h.shape
    n = w.shape[1]
    width = D_MODEL
    nblk = n // width
    lane = jnp.arange(LANES)
    seg = (lane[:, None] // head_dim == lane[None, :] // head_dim).astype(BF16)
    return pl.pallas_call(
        functools.partial(_proj_qkv_kernel, head_dim=head_dim),
        grid=(t // tm, nblk),
        in_specs=[
            pl.BlockSpec((tm, d), lambda i, j: (i, 0)),
            pl.BlockSpec((1, d), lambda i, j: (0, 0)),
            pl.BlockSpec((d, width), lambda i, j: (0, j)),
            pl.BlockSpec((None, 1, width), lambda i, j: (j, 0, 0)),
            pl.BlockSpec((LANES, LANES), lambda i, j: (0, 0)),
        ],
        out_specs=pl.BlockSpec((tm, width), lambda i, j: (i, j)),
        out_shape=jax.ShapeDtypeStruct((t, n), BF16),
        scratch_shapes=[pltpu.VMEM((tm, d), BF16), pltpu.VMEM((tm, width), F32)],
        compiler_params=_params("parallel", "arbitrary"),
        name="proj_qkv",
    )(h, norm_g.reshape(1, d), w, gains.reshape(nblk, 1, width), seg)


def _sb_attn_kernel(q_ref, k_ref, v_ref, tri_ref, o_ref, *, blk):
    i = pl.program_id(2)
    lane = lax.broadcasted_iota(jnp.int32, (1, LANES), 1)
    row = lax.broadcasted_iota(jnp.int32, (blk, blk), 0)
    col = lax.broadcasted_iota(jnp.int32, (blk, blk), 1)
    strictly_past = col < row
    q = q_ref[...]
    out = jnp.zeros((blk, LANES), F32)
    for head in range(LANES // SB_HEAD_DIM):
        head_lanes = (lane // SB_HEAD_DIM) == head
        qh = jnp.where(head_lanes, q, jnp.zeros_like(q))

        def tile(j, carry, diagonal):
            acc, later = carry
            start = pl.multiple_of(j * blk, blk)
            z = _dot_t(qh, k_ref[pl.ds(start, blk), :])
            if diagonal:
                z = jnp.where(strictly_past, z, MASKED)
            sp = jnp.maximum(z, 0.0) + jnp.log(1.0 + jnp.exp(-jnp.abs(z)))
            within = _split_dot(sp, tri_ref[...])
            a = jnp.exp(z - within - later)
            acc = acc + jnp.dot(a.astype(BF16), v_ref[pl.ds(start, blk), :], preferred_element_type=F32)
            return acc, later + within[:, 0:1]

        carry = tile(i, (jnp.zeros((blk, LANES), F32), jnp.zeros((blk, 1), F32)), True)
        acc, _ = lax.fori_loop(0, i, lambda t, c: tile(i - 1 - t, c, False), carry)
        out = out + jnp.where(head_lanes, acc, 0.0)
    o_ref[...] = out.astype(BF16)


def _sb_attention(qkv, batch, seq, blk=256):
    pairs = D_MODEL // LANES
    nq = seq // blk
    qkv3 = qkv.reshape(batch, seq, 3 * D_MODEL)
    idx = jnp.arange(blk)
    tri = (idx[:, None] >= idx[None, :]).astype(BF16)
    out = pl.pallas_call(
        functools.partial(_sb_attn_kernel, blk=blk),
        grid=(batch, pairs, nq),
        in_specs=[
            pl.BlockSpec((None, blk, LANES), lambda b, p, i: (b, i, p)),
            pl.BlockSpec((None, seq, LANES), lambda b, p, i: (b, 0, pairs + p)),
            pl.BlockSpec((None, seq, LANES), lambda b, p, i: (b, 0, 2 * pairs + p)),
            pl.BlockSpec((blk, blk), lambda b, p, i: (0, 0)),
        ],
        out_specs=pl.BlockSpec((None, blk, LANES), lambda b, p, i: (b, i, p)),
        out_shape=jax.ShapeDtypeStruct((batch, seq, D_MODEL), BF16),
        compiler_params=_params("parallel", "parallel", "arbitrary"),
        name="sb_attention",
    )(qkv3, qkv3, qkv3, tri)
    return out.reshape(batch * seq, D_MODEL)


def _out_proj_kernel(o_ref, w_ref, h_ref, out_ref):
    out_ref[...] = h_ref[...] + jnp.dot(o_ref[...], w_ref[...], preferred_element_type=F32)


def _out_proj(o, w, h, tm=1024):
    t, d = h.shape
    return pl.pallas_call(
        _out_proj_kernel,
        grid=(t // tm,),
        in_specs=[
            pl.BlockSpec((tm, o.shape[1]), lambda i: (i, 0)),
            pl.BlockSpec(w.shape, lambda i: (0, 0)),
            pl.BlockSpec((tm, d), lambda i: (i, 0)),
        ],
        out_specs=pl.BlockSpec((tm, d), lambda i: (i, 0)),
        out_shape=jax.ShapeDtypeStruct((t, d), F32),
        compiler_params=_params("parallel"),
        name="out_proj",
    )(o, w, h)


def _mlp_kernel(h_ref, g_ref, wu_ref, wd_ref, out_ref, xn_ref, acc_ref):
    f = pl.program_id(1)

    @pl.when(f == 0)
    def _():
        xn_ref[...] = _rms(h_ref[...], g_ref[...]).astype(BF16)
        acc_ref[...] = jnp.zeros_like(acc_ref)

    u = jnp.maximum(jnp.dot(xn_ref[...], wu_ref[...], preferred_element_type=F32), 0.0)
    acc_ref[...] += jnp.dot((u * u).astype(BF16), wd_ref[...], preferred_element_type=F32)

    @pl.when(f == pl.num_programs(1) - 1)
    def _():
        out_ref[...] = h_ref[...] + acc_ref[...]


def _mlp(h, norm_g, w_up, w_down, tm=1024, fc=512):
    t, d = h.shape
    ff = w_up.shape[1]
    return pl.pallas_call(
        _mlp_kernel,
        grid=(t // tm, ff // fc),
        in_specs=[
            pl.BlockSpec((tm, d), lambda i, f: (i, 0)),
            pl.BlockSpec((1, d), lambda i, f: (0, 0)),
            pl.BlockSpec((d, fc), lambda i, f: (0, f)),
            pl.BlockSpec((fc, d), lambda i, f: (f, 0)),
        ],
        out_specs=pl.BlockSpec((tm, d), lambda i, f: (i, 0)),
        out_shape=jax.ShapeDtypeStruct((t, d), F32),
        scratch_shapes=[pltpu.VMEM((tm, d), BF16), pltpu.VMEM((tm, d), F32)],
        compiler_params=_params("parallel", "arbitrary"),
        name="mlp",
    )(h, norm_g.reshape(1, d), w_up, w_down)


def _dil_attn_kernel(q_ref, kc_ref, kp_ref, vc_ref, vp_ref, o_ref, lse_ref, *, bias_per_step, max_steps):
    n = pl.program_id(2)
    blk = DIL_BLOCK
    qi = lax.broadcasted_iota(jnp.int32, (blk, 2 * blk), 0)
    kj = lax.broadcasted_iota(jnp.int32, (blk, 2 * blk), 1)
    steps = blk + qi - kj
    valid = (steps >= 0) & (steps <= max_steps) & ((kj >= blk) | (n > 0))
    steps_f = steps.astype(F32)
    lane = lax.broadcasted_iota(jnp.int32, (1, LANES), 1)
    lse_tile = jnp.zeros((blk, LANES), F32)
    for head in range(DIL_HEADS):
        sl = slice(head * DIL_HEAD_DIM, (head + 1) * DIL_HEAD_DIM)
        keys = jnp.concatenate([kp_ref[:, sl], kc_ref[:, sl]], axis=0)
        vals = jnp.concatenate([vp_ref[:, sl], vc_ref[:, sl]], axis=0)
        z = _dot_t(q_ref[:, sl], keys)
        z = jnp.where(valid, z - bias_per_step[head] * steps_f, MASKED)
        m = jnp.max(z, axis=-1, keepdims=True)
        p = jnp.exp(z - m)
        denom = jnp.sum(p, axis=-1, keepdims=True)
        o_ref[:, sl] = jnp.dot(p.astype(BF16), vals, preferred_element_type=F32) / denom
        lse_tile = jnp.where(lane == head, m + jnp.log(denom), lse_tile)
    lse_ref[...] = lse_tile


def _dil_attention(qkv, group, batch, seq):
    window, dilation = DIL_GROUPS[group]
    n_groups = len(DIL_GROUPS)
    blk = DIL_BLOCK
    length = seq // dilation
    assert length % blk == 0
    nb = length // blk
    width = DIL_HEADS * DIL_HEAD_DIM
    ncol = 3 * n_groups
    slopes = [2.0 ** (-8.0 * (group * DIL_HEADS + hd + 1) / (n_groups * DIL_HEADS)) for hd in range(DIL_HEADS)]
    view = qkv.reshape(batch, length, dilation * ncol * width)

    def col(offset):
        return lambda b, r, n: (b, n, r * ncol + 3 * group + offset)

    def col_prev(offset):
        return lambda b, r, n: (b, jnp.maximum(n - 1, 0), r * ncol + 3 * group + offset)

    o, lse = pl.pallas_call(
        functools.partial(_dil_attn_kernel,
                          bias_per_step=tuple(s * dilation for s in slopes),
                          max_steps=window // dilation),
        grid=(batch, dilation, nb),
        in_specs=[
            pl.BlockSpec((None, blk, width), col(0)),
            pl.BlockSpec((None, blk, width), col(1)),
            pl.BlockSpec((None, blk, width), col_prev(1)),
            pl.BlockSpec((None, blk, width), col(2)),
            pl.BlockSpec((None, blk, width), col_prev(2)),
        ],
        out_specs=[
            pl.BlockSpec((None, blk, width), lambda b, r, n: (b, n, r)),
            pl.BlockSpec((None, blk, LANES), lambda b, r, n: (b, n, r)),
        ],
        out_shape=[
            jax.ShapeDtypeStruct((batch, length, dilation * width), F32),
            jax.ShapeDtypeStruct((batch, length, dilation * LANES), F32),
        ],
        compiler_params=_params("parallel", "parallel", "arbitrary"),
        name="dil_attention_g%d" % group,
    )(view, view, view, view, view)
    return o.reshape(batch * seq, width), lse.reshape(batch * seq, LANES)


def _dil_merge_kernel(o0_ref, o1_ref, o2_ref, l0_ref, l1_ref, l2_ref, w_ref, h_ref, out_ref, mrg_ref):
    lses = (l0_ref[...], l1_ref[...], l2_ref[...])
    outs = (o0_ref, o1_ref, o2_ref)
    m = jnp.maximum(jnp.maximum(lses[0], lses[1]), lses[2])
    es = [jnp.exp(l - m) for l in lses]
    total = es[0] + es[1] + es[2]
    alphas = [e / total for e in es]
    for head in range(DIL_HEADS):
        sl = slice(head * DIL_HEAD_DIM, (head + 1) * DIL_HEAD_DIM)
        merged = sum(alphas[g][:, head:head + 1] * outs[g][:, sl] for g in range(len(outs)))
        mrg_ref[:, sl] = merged.astype(BF16)
    out_ref[...] = h_ref[...] + jnp.dot(mrg_ref[...], w_ref[...], preferred_element_type=F32)


def _dil_merge_out_proj(outs, lses, w, h, tm=512):
    t, d = h.shape
    tile = lambda cols: pl.BlockSpec((tm, cols), lambda i: (i, 0))
    return pl.pallas_call(
        _dil_merge_kernel,
        grid=(t // tm,),
        in_specs=[tile(d)] * 3 + [tile(LANES)] * 3 + [pl.BlockSpec(w.shape, lambda i: (0, 0)), tile(d)],
        out_specs=tile(d),
        out_shape=jax.ShapeDtypeStruct((t, d), F32),
        scratch_shapes=[pltpu.VMEM((tm, d), BF16)],
        compiler_params=_params("parallel"),
        name="dil_merge_out_proj",
    )(*outs, *lses, w, h)


def _mla_proj_kernel(x_ref, g_ref, win_ref, qa_ref, kva_ref, wq_ref, wk_ref, wv_ref, qg_ref, kg_ref,
                     cos_ref, sin_up_ref, sin_dn_ref, ones_ref, q_out, k_out, v_out, qp_ref, kp_ref):
    xn = _rms(x_ref[...], g_ref[...]).astype(BF16)
    c = jnp.dot(xn, win_ref[...], preferred_element_type=F32)
    cq = _rms(c[:, :MLA_Q_RANK], qa_ref[...]).astype(BF16)
    ckv = _rms(c[:, MLA_Q_RANK:MLA_Q_RANK + MLA_KV_RANK], kva_ref[...]).astype(BF16)
    shared_rope = c[:, MLA_Q_RANK + MLA_KV_RANK:]
    qp_ref[...] = jnp.dot(cq, wq_ref[...], preferred_element_type=F32)
    kp_ref[...] = jnp.dot(ckv, wk_ref[...], preferred_element_type=F32)
    v_out[...] = jnp.dot(ckv, wv_ref[...], preferred_element_type=F32).astype(BF16)
    cos = cos_ref[...]
    sin_up = sin_up_ref[...]
    sin_dn = sin_dn_ref[...]
    half = MLA_ROPE // 2

    def norm_rope(x, gain):
        ms = _split_dot(x * x, ones_ref[...]) * (1.0 / MLA_QK)
        y = x * lax.rsqrt(ms + EPS) * gain
        return y * cos + pltpu.roll(y, half, 1) * sin_up + pltpu.roll(y, LANES - half, 1) * sin_dn

    for head in range(MLA_HEADS):
        sl = slice(head * LANES, (head + 1) * LANES)
        q_out[:, sl] = norm_rope(qp_ref[:, sl], qg_ref[...]).astype(BF16)
        k_out[:, sl] = norm_rope(kp_ref[:, sl] + shared_rope, kg_ref[...]).astype(BF16)


def _pad_heads(w, heads, src_lo, src_hi, src_width):
    k = w.shape[0]
    w = w.reshape(k, heads, src_width)[:, :, src_lo:src_hi]
    w = jnp.pad(w, ((0, 0), (0, 0), (0, LANES - (src_hi - src_lo))))
    return w.reshape(k, heads * LANES)


def _mla_proj(h, norm_g, w_in, q_a_gain, kv_a_gain, w_q_b, w_kv_b, q_gain, k_gain, seq, tm=512):
    t, d = h.shape
    latent = MLA_Q_RANK + MLA_KV_RANK
    w_in_p = jnp.concatenate([
        w_in[:, :latent],
        jnp.zeros((d, MLA_NOPE), w_in.dtype),
        w_in[:, latent:],
        jnp.zeros((d, LANES - MLA_QK), w_in.dtype)], axis=1).astype(BF16)
    wq = _pad_heads(w_q_b, MLA_HEADS, 0, MLA_QK, MLA_QK).astype(BF16)
    wk = _pad_heads(w_kv_b, MLA_HEADS, 0, MLA_NOPE, MLA_NOPE + MLA_V).astype(BF16)
    wv = w_kv_b.reshape(MLA_KV_RANK, MLA_HEADS, MLA_NOPE + MLA_V)[:, :, MLA_NOPE:]
    wv = wv.reshape(MLA_KV_RANK, MLA_HEADS * MLA_V).astype(BF16)
    pad = jnp.zeros((LANES - MLA_QK,), F32)
    qg = (jnp.concatenate([q_gain, pad]) * (1.0 / math.sqrt(MLA_QK))).reshape(1, LANES)
    kg = jnp.concatenate([k_gain, pad]).reshape(1, LANES)
    half = MLA_ROPE // 2
    inv = ROPE_THETA ** (-jnp.arange(half, dtype=F32) / half)
    ang = jnp.arange(seq, dtype=F32)[:, None] * inv[None, :]
    zeros = jnp.zeros((seq, half), F32)
    cos = jnp.concatenate([jnp.ones((seq, MLA_NOPE), F32), jnp.cos(ang), jnp.cos(ang),
                           jnp.zeros((seq, LANES - MLA_QK), F32)], axis=1)
    sin_up = jnp.concatenate([jnp.zeros((seq, MLA_NOPE), F32), zeros, jnp.sin(ang),
                              jnp.zeros((seq, LANES - MLA_QK), F32)], axis=1)
    sin_dn = jnp.concatenate([jnp.zeros((seq, MLA_NOPE), F32), -jnp.sin(ang), zeros,
                              jnp.zeros((seq, LANES - MLA_QK), F32)], axis=1)
    ones = jnp.ones((LANES, LANES), BF16)
    per_seq = seq // tm
    full = lambda a: pl.BlockSpec(a.shape, lambda i: (0,) * a.ndim)
    table = pl.BlockSpec((tm, LANES), lambda i: (i % per_seq, 0))
    consts = (norm_g.reshape(1, d), w_in_p, q_a_gain.reshape(1, -1), kv_a_gain.reshape(1, -1), wq, wk, wv, qg, kg)
    qk_cols = MLA_HEADS * LANES
    return pl.pallas_call(
        _mla_proj_kernel,
        grid=(t // tm,),
        in_specs=[pl.BlockSpec((tm, d), lambda i: (i, 0))] + [full(a) for a in consts]
                 + [table, table, table, full(ones)],
        out_specs=[
            pl.BlockSpec((tm, qk_cols), lambda i: (i, 0)),
            pl.BlockSpec((tm, qk_cols), lambda i: (i, 0)),
            pl.BlockSpec((tm, MLA_HEADS * MLA_V), lambda i: (i, 0)),
        ],
        out_shape=[
            jax.ShapeDtypeStruct((t, qk_cols), BF16),
            jax.ShapeDtypeStruct((t, qk_cols), BF16),
            jax.ShapeDtypeStruct((t, MLA_HEADS * MLA_V), BF16),
        ],
        scratch_shapes=[pltpu.VMEM((tm, qk_cols), F32), pltpu.VMEM((tm, qk_cols), F32)],
        compiler_params=_params("parallel"),
        name="mla_proj",
    )(h, *consts, cos, sin_up, sin_dn, ones)


def _mla_attn_kernel(q_ref, k_ref, v_ref, o_ref, *, blk):
    i = pl.program_id(2)
    lane = lax.broadcasted_iota(jnp.int32, (1, LANES), 1)
    row = lax.broadcasted_iota(jnp.int32, (blk, blk), 0)
    col = lax.broadcasted_iota(jnp.int32, (blk, blk), 1)
    causal = col <= row
    out = jnp.zeros((blk, LANES), F32)
    for head in range(LANES // MLA_V):
        sl = slice(head * LANES, (head + 1) * LANES)
        qh = q_ref[:, sl]

        def tile(j, carry, diagonal):
            m, l, acc = carry
            start = pl.multiple_of(j * blk, blk)
            s = _dot_t(qh, k_ref[pl.ds(start, blk), sl])
            if diagonal:
                s = jnp.where(causal, s, MASKED)
            m_new = jnp.maximum(m, jnp.max(s, axis=-1, keepdims=True))
            p = jnp.exp(s - m_new)
            corr = jnp.exp(m - m_new)
            l = corr * l + jnp.sum(p, axis=-1, keepdims=True)
            acc = corr * acc + jnp.dot(p.astype(BF16), v_ref[pl.ds(start, blk), :], preferred_element_type=F32)
            return m_new, l, acc

        init = (jnp.full((blk, 1), MASKED, F32), jnp.zeros((blk, 1), F32), jnp.zeros((blk, LANES), F32))
        carry = tile(i, init, True)
        _, l, acc = lax.fori_loop(0, i, lambda t, c: tile(i - 1 - t, c, False), carry)
        out = out + jnp.where((lane // MLA_V) == head, acc / l, 0.0)
    o_ref[...] = out.astype(BF16)


def _mla_attention(q, k, v, batch, seq, blk=256):
    pairs = MLA_HEADS * MLA_V // LANES
    q3 = q.reshape(batch, seq, -1)
    k3 = k.reshape(batch, seq, -1)
    v3 = v.reshape(batch, seq, -1)
    out = pl.pallas_call(
        functools.partial(_mla_attn_kernel, blk=blk),
        grid=(batch, pairs, seq // blk),
        in_specs=[
            pl.BlockSpec((None, blk, 2 * LANES), lambda b, p, i: (b, i, p)),
            pl.BlockSpec((None, seq, 2 * LANES), lambda b, p, i: (b, 0, p)),
            pl.BlockSpec((None, seq, LANES), lambda b, p, i: (b, 0, p)),
        ],
        out_specs=pl.BlockSpec((None, blk, LANES), lambda b, p, i: (b, i, p)),
        out_shape=jax.ShapeDtypeStruct((batch, seq, MLA_HEADS * MLA_V), BF16),
        compiler_params=_params("parallel", "parallel", "arbitrary"),
        name="mla_attention",
    )(q3, k3, v3)
    return out.reshape(batch * seq, MLA_HEADS * MLA_V)


def _sb_layer(h, batch, seq, mix_norm, w_in, q_norm, k_norm, w_out):
    scale = 1.0 / math.sqrt(SB_HEAD_DIM)
    gains = jnp.stack([jnp.tile(q_norm, SB_HEADS) * scale, jnp.tile(k_norm, SB_HEADS), jnp.ones((D_MODEL,), F32)])
    qkv = _proj_qkv(h, mix_norm, w_in.astype(BF16), gains, SB_HEAD_DIM)
    o = _sb_attention(qkv, batch, seq)
    return _out_proj(o, w_out.astype(BF16), h)


def _dil_layer(h, batch, seq, mix_norm, w_in, q_norm, k_norm, w_out):
    scale = DIL_HEAD_DIM ** -0.5
    rows = []
    for g in range(len(DIL_GROUPS)):
        rows += [jnp.tile(q_norm[g], DIL_HEADS) * scale, jnp.tile(k_norm[g], DIL_HEADS), jnp.ones((D_MODEL,), F32)]
    qkv = _proj_qkv(h, mix_norm, w_in.astype(BF16), jnp.stack(rows), DIL_HEAD_DIM)
    outs, lses = zip(*[_dil_attention(qkv, g, batch, seq) for g in range(len(DIL_GROUPS))])
    return _dil_merge_out_proj(outs, lses, w_out.astype(BF16), h)


def _mla_layer(h, batch, seq, mix_norm, w_in, q_a_norm, kv_a_norm, w_q_b, w_kv_b, q_norm, k_norm, w_out):
    q, k, v = _mla_proj(h, mix_norm, w_in, q_a_norm, kv_a_norm, w_q_b, w_kv_b, q_norm, k_norm, seq)
    o = _mla_attention(q, k, v, batch, seq)
    return _out_proj(o, w_out.astype(BF16), h)


def _trunk(x, layers):
    batch, seq, d = x.shape
    h = x.reshape(batch * seq, d)
    mixers = (_sb_layer, _dil_layer, _mla_layer)
    for idx, (mix_norm, mix_params, mlp_norm, w_up, w_down) in enumerate(layers):
        h = mixers[idx % len(mixers)](h, batch, seq, mix_norm, *mix_params)
        h = _mlp(h, mlp_norm, w_up.astype(BF16), w_down.astype(BF16))
    return h.reshape(batch, seq, d)


def kernel(x, l0_mix_norm, l0_sb_w_in, l0_sb_q_norm, l0_sb_k_norm, l0_sb_w_out, l0_mlp_norm, l0_mlp_w_up, l0_mlp_w_down, l1_mix_norm, l1_dil_w_in, l1_dil_q_norm, l1_dil_k_norm, l1_dil_w_out, l1_mlp_norm, l1_mlp_w_up, l1_mlp_w_down, l2_mix_norm, l2_mla_w_in, l2_mla_q_a_norm, l2_mla_kv_a_norm, l2_mla_w_q_b, l2_mla_w_kv_b, l2_mla_q_norm, l2_mla_k_norm, l2_mla_w_out, l2_mlp_norm, l2_mlp_w_up, l2_mlp_w_down, l3_mix_norm, l3_sb_w_in, l3_sb_q_norm, l3_sb_k_norm, l3_sb_w_out, l3_mlp_norm, l3_mlp_w_up, l3_mlp_w_down):
    layers = [
        (l0_mix_norm, (l0_sb_w_in, l0_sb_q_norm, l0_sb_k_norm, l0_sb_w_out),
         l0_mlp_norm, l0_mlp_w_up, l0_mlp_w_down),
        (l1_mix_norm, (l1_dil_w_in, l1_dil_q_norm, l1_dil_k_norm, l1_dil_w_out),
         l1_mlp_norm, l1_mlp_w_up, l1_mlp_w_down),
        (l2_mix_norm, (l2_mla_w_in, l2_mla_q_a_norm, l2_mla_kv_a_norm, l2_mla_w_q_b,
                       l2_mla_w_kv_b, l2_mla_q_norm, l2_mla_k_norm, l2_mla_w_out),
         l2_mlp_norm, l2_mlp_w_up, l2_mlp_w_down),
        (l3_mix_norm, (l3_sb_w_in, l3_sb_q_norm, l3_sb_k_norm, l3_sb_w_out),
         l3_mlp_norm, l3_mlp_w_up, l3_mlp_w_down),
    ]
    return _trunk(x, layers)
```

```python
import functools
import math

import jax
import jax.numpy as jnp
from jax import lax
from jax.experimental import pallas as pl
from jax.experimental.pallas import tpu as pltpu

F32 = jnp.float32
BF16 = jnp.bfloat16

EPS = 1e-6
LANES = 128
MASKED = -1e30
LOG2_E = math.log2(math.e)
VMEM_LIMIT_BYTES = 48 * 1024 * 1024

D_MODEL = 1024
SB_HEADS = 16
SB_HEAD_DIM = 64
DIL_GROUPS = ((128, 1), (512, 4), (2048, 16))
DIL_HEADS = 8
DIL_HEAD_DIM = 128
DIL_BLOCK = 128
MLA_HEADS = 16
MLA_NOPE = 64
MLA_ROPE = 32
MLA_V = 64
MLA_QK = MLA_NOPE + MLA_ROPE
MLA_Q_RANK = 384
MLA_KV_RANK = 256
ROPE_THETA = 10000.0


def _params(*semantics):
    return pltpu.CompilerParams(dimension_semantics=semantics, vmem_limit_bytes=VMEM_LIMIT_BYTES)


def _rms(x, g):
    ms = jnp.mean(x * x, axis=-1, keepdims=True)
    return x * lax.rsqrt(ms + EPS) * g


def _split_dot(x, w):
    hi = x.astype(BF16)
    lo = (x - hi.astype(F32)).astype(BF16)
    return (jnp.dot(hi, w, preferred_element_type=F32)
            + jnp.dot(lo, w, preferred_element_type=F32))


def _dot_t(a, b):
    return lax.dot_general(a, b, (((1,), (1,)), ((), ())), preferred_element_type=F32)


def _proj_qkv_kernel(x_ref, g_ref, w_ref, gain_ref, seg_ref, o_ref, xn_ref, y_ref, *, head_dim, dilation):
    j = pl.program_id(1)
    tm, width = xn_ref.shape[0], w_ref.shape[1]
    slabs = width // LANES

    @pl.when(j == 0)
    def _():
        xn_ref[...] = _rms(x_ref[...], g_ref[...]).astype(BF16)

    y = jnp.dot(xn_ref[...], w_ref[...], preferred_element_type=F32)
    for c in range(slabs):
        y_ref[c] = y[:, c * LANES:(c + 1) * LANES]

    @pl.when(j < 2)
    def _():
        for c in range(slabs):
            yc = y_ref[c]
            ms = _split_dot(yc * yc, seg_ref[...]) * (1.0 / head_dim)
            y_ref[c] = yc * lax.rsqrt(ms + EPS) * gain_ref[:, c * LANES:(c + 1) * LANES]

    for block in range(3):
        @pl.when(j == block)
        def _(block=block):
            for r in range(dilation):
                for c in range(slabs):
                    rows = y_ref[c] if dilation == 1 else y_ref[c, pl.ds(r, tm // dilation, stride=dilation), :]
                    lo = (r * 3 + block) * width + c * LANES
                    o_ref[:, lo:lo + LANES] = rows.astype(BF16)


def _proj_qkv(h, norm_g, w, gains, head_dim, dilation=1, tm=512):
    t, d = h.shape
    width = D_MODEL
    assert w.shape == (d, 3 * width) and tm % (16 * dilation) == 0
    lane = jnp.arange(LANES)
    seg = (lane[:, None] // head_dim == lane[None, :] // head_dim).astype(BF16)
    return pl.pallas_call(
        functools.partial(_proj_qkv_kernel, head_dim=head_dim, dilation=dilation),
        grid=(t // tm, 3),
        in_specs=[
            pl.BlockSpec((tm, d), lambda i, j: (i, 0)),
            pl.BlockSpec((1, d), lambda i, j: (0, 0)),
            pl.BlockSpec((d, width), lambda i, j: (0, j)),
            pl.BlockSpec((None, 1, width), lambda i, j: (jnp.minimum(j, 1), 0, 0)),
            pl.BlockSpec((LANES, LANES), lambda i, j: (0, 0)),
        ],
        out_specs=pl.BlockSpec((tm // dilation, dilation * 3 * width), lambda i, j: (i, 0)),
        out_shape=jax.ShapeDtypeStruct((t // dilation, dilation * 3 * width), BF16),
        scratch_shapes=[pltpu.VMEM((tm, d), BF16), pltpu.VMEM((width // LANES, tm, LANES), F32)],
        compiler_params=_params("parallel", "arbitrary"),
        name="proj_qkv_d%d" % dilation,
    )(h, norm_g.reshape(1, d), w, gains.reshape(2, 1, width), seg)


def _sb_attn_kernel(q_ref, k_ref, v_ref, tri_ref, o_ref, *, blk):
    i = pl.program_id(2)
    lane = lax.broadcasted_iota(jnp.int32, (1, LANES), 1)
    row = lax.broadcasted_iota(jnp.int32, (blk, blk), 0)
    col = lax.broadcasted_iota(jnp.int32, (blk, blk), 1)
    strictly_past = col < row
    q = q_ref[...]
    heads = LANES // SB_HEAD_DIM
    head_lanes = [(lane // SB_HEAD_DIM) == head for head in range(heads)]
    qs = [jnp.where(hl, q, jnp.zeros_like(q)) for hl in head_lanes]

    def tiles(js, masks, state):
        starts = [pl.multiple_of(jnp.maximum(j, 0) * blk, blk) for j in js]
        keys = [k_ref[pl.ds(start, blk), :] for start in starts]
        vals = [v_ref[pl.ds(start, blk), :] for start in starts]
        pairs = [(t, head) for t in range(len(js)) for head in range(heads)]
        zs = {}
        for t, head in pairs:
            z = _dot_t(qs[head], keys[t])
            zs[t, head] = z if masks[t] is None else jnp.where(masks[t], z, MASKED)
        sps = {p: jnp.maximum(zs[p], 0.0) + jnp.log2(1.0 + jnp.exp2(-jnp.abs(zs[p]))) for p in pairs}
        withins = {p: _split_dot(sps[p], tri_ref[...]) for p in pairs}
        later = [state[head][1] for head in range(heads)]
        probs = {}
        for t, head in pairs:
            probs[t, head] = jnp.exp2(zs[t, head] - withins[t, head] - later[head]).astype(BF16)
            later[head] = later[head] + withins[t, head][:, 0:1]
        acc = [state[head][0] for head in range(heads)]
        for t, head in pairs:
            acc[head] = acc[head] + jnp.dot(probs[t, head], vals[t], preferred_element_type=F32)
        return tuple((acc[head], later[head]) for head in range(heads))

    state = tuple((jnp.zeros((blk, LANES), F32), jnp.zeros((blk, 1), F32)) for _ in range(heads))
    state = tiles([i, i - 1], [strictly_past, i > 0], state)
    remaining = jnp.maximum(i - 1, 0)

    def pair(t, s):
        j = i - 2 - 2 * t
        return tiles([j, j - 1], [None, j > 0], s)

    state = lax.fori_loop(0, lax.shift_right_logical(remaining + 1, 1), pair, state)
    out = sum(jnp.where(head_lanes[head], state[head][0], 0.0) for head in range(heads))
    o_ref[...] = out.astype(BF16)


def _sb_attention(qkv, batch, seq, blk=256):
    pairs = D_MODEL // LANES
    nq = seq // blk
    qkv3 = qkv.reshape(batch, seq, 3 * D_MODEL)
    idx = jnp.arange(blk)
    tri = (idx[:, None] >= idx[None, :]).astype(BF16)
    out = pl.pallas_call(
        functools.partial(_sb_attn_kernel, blk=blk),
        grid=(batch, pairs, nq),
        in_specs=[
            pl.BlockSpec((None, blk, LANES), lambda b, p, i: (b, i, p)),
            pl.BlockSpec((None, seq, LANES), lambda b, p, i: (b, 0, pairs + p)),
            pl.BlockSpec((None, seq, LANES), lambda b, p, i: (b, 0, 2 * pairs + p)),
            pl.BlockSpec((blk, blk), lambda b, p, i: (0, 0)),
        ],
        out_specs=pl.BlockSpec((None, blk, LANES), lambda b, p, i: (b, i, p)),
        out_shape=jax.ShapeDtypeStruct((batch, seq, D_MODEL), BF16),
        compiler_params=_params("parallel", "parallel", "arbitrary"),
        name="sb_attention",
    )(qkv3, qkv3, qkv3, tri)
    return out.reshape(batch * seq, D_MODEL)


def _out_proj_kernel(o_ref, w_ref, h_ref, out_ref):
    out_ref[...] = h_ref[...] + jnp.dot(o_ref[...], w_ref[...], preferred_element_type=F32)


def _out_proj(o, w, h, tm=1024):
    t, d = h.shape
    return pl.pallas_call(
        _out_proj_kernel,
        grid=(t // tm,),
        in_specs=[
            pl.BlockSpec((tm, o.shape[1]), lambda i: (i, 0)),
            pl.BlockSpec(w.shape, lambda i: (0, 0)),
            pl.BlockSpec((tm, d), lambda i: (i, 0)),
        ],
        out_specs=pl.BlockSpec((tm, d), lambda i: (i, 0)),
        out_shape=jax.ShapeDtypeStruct((t, d), F32),
        compiler_params=_params("parallel"),
        name="out_proj",
    )(o, w, h)


def _mlp_kernel(h_ref, g_ref, wu_ref, wd_ref, out_ref, xn_ref, acc_ref):
    f = pl.program_id(1)

    @pl.when(f == 0)
    def _():
        xn_ref[...] = _rms(h_ref[...], g_ref[...]).astype(BF16)
        acc_ref[...] = jnp.zeros_like(acc_ref)

    u = jnp.maximum(jnp.dot(xn_ref[...], wu_ref[...], preferred_element_type=F32), 0.0)
    acc_ref[...] += jnp.dot((u * u).astype(BF16), wd_ref[...], preferred_element_type=F32)

    @pl.when(f == pl.num_programs(1) - 1)
    def _():
        out_ref[...] = h_ref[...] + acc_ref[...]


def _mlp(h, norm_g, w_up, w_down, tm=1024, fc=512):
    t, d = h.shape
    ff = w_up.shape[1]
    return pl.pallas_call(
        _mlp_kernel,
        grid=(t // tm, ff // fc),
        in_specs=[
            pl.BlockSpec((tm, d), lambda i, f: (i, 0)),
            pl.BlockSpec((1, d), lambda i, f: (0, 0)),
            pl.BlockSpec((d, fc), lambda i, f: (0, f)),
            pl.BlockSpec((fc, d), lambda i, f: (f, 0)),
        ],
        out_specs=pl.BlockSpec((tm, d), lambda i, f: (i, 0)),
        out_shape=jax.ShapeDtypeStruct((t, d), F32),
        scratch_shapes=[pltpu.VMEM((tm, d), BF16), pltpu.VMEM((tm, d), F32)],
        compiler_params=_params("parallel", "arbitrary"),
        name="mlp",
    )(h, norm_g.reshape(1, d), w_up, w_down)


def _dil_attn_kernel(q_ref, kc_ref, kp_ref, vc_ref, vp_ref, o_ref, lse_ref, *, bias_per_step, max_steps):
    n = pl.program_id(2)
    blk = DIL_BLOCK
    qi = lax.broadcasted_iota(jnp.int32, (blk, 2 * blk), 0)
    kj = lax.broadcasted_iota(jnp.int32, (blk, 2 * blk), 1)
    steps = blk + qi - kj
    valid = (steps >= 0) & (steps <= max_steps) & ((kj >= blk) | (n > 0))
    steps_f = steps.astype(F32)
    lane = lax.broadcasted_iota(jnp.int32, (1, LANES), 1)
    lse_tile = jnp.zeros((blk, LANES), F32)
    for head in range(DIL_HEADS):
        sl = slice(head * DIL_HEAD_DIM, (head + 1) * DIL_HEAD_DIM)
        keys = jnp.concatenate([kp_ref[:, sl], kc_ref[:, sl]], axis=0)
        vals = jnp.concatenate([vp_ref[:, sl], vc_ref[:, sl]], axis=0)
        z = _dot_t(q_ref[:, sl], keys)
        z = jnp.where(valid, z - bias_per_step[head] * steps_f, MASKED)
        m = jnp.max(z, axis=-1, keepdims=True)
        p = jnp.exp(z - m)
        denom = jnp.sum(p, axis=-1, keepdims=True)
        o_ref[:, sl] = jnp.dot(p.astype(BF16), vals, preferred_element_type=F32) / denom
        lse_tile = jnp.where(lane == head, m + jnp.log(denom), lse_tile)
    lse_ref[...] = lse_tile


def _dil_attention(qkv, group, batch, seq):
    window, dilation = DIL_GROUPS[group]
    n_groups = len(DIL_GROUPS)
    blk = DIL_BLOCK
    length = seq // dilation
    assert length % blk == 0
    nb = length // blk
    width = DIL_HEADS * DIL_HEAD_DIM
    slopes = [2.0 ** (-8.0 * (group * DIL_HEADS + hd + 1) / (n_groups * DIL_HEADS)) for hd in range(DIL_HEADS)]
    view = qkv.reshape(batch, length, dilation * 3 * width)

    def col(offset):
        return lambda b, r, n: (b, n, r * 3 + offset)

    def col_prev(offset):
        return lambda b, r, n: (b, jnp.maximum(n - 1, 0), r * 3 + offset)

    o, lse = pl.pallas_call(
        functools.partial(_dil_attn_kernel,
                          bias_per_step=tuple(s * dilation for s in slopes),
                          max_steps=window // dilation),
        grid=(batch, dilation, nb),
        in_specs=[
            pl.BlockSpec((None, blk, width), col(0)),
            pl.BlockSpec((None, blk, width), col(1)),
            pl.BlockSpec((None, blk, width), col_prev(1)),
            pl.BlockSpec((None, blk, width), col(2)),
            pl.BlockSpec((None, blk, width), col_prev(2)),
        ],
        out_specs=[
            pl.BlockSpec((None, blk, width), lambda b, r, n: (b, n, r)),
            pl.BlockSpec((None, blk, LANES), lambda b, r, n: (b, n, r)),
        ],
        out_shape=[
            jax.ShapeDtypeStruct((batch, length, dilation * width), F32),
            jax.ShapeDtypeStruct((batch, length, dilation * LANES), F32),
        ],
        compiler_params=_params("parallel", "parallel", "arbitrary"),
        name="dil_attention_g%d" % group,
    )(view, view, view, view, view)
    rows = batch * length
    return o.reshape(rows, dilation * width), lse.reshape(rows, dilation * LANES)


def _dil_merge_kernel(o0_ref, o1_ref, o2_ref, l0_ref, l1_ref, l2_ref, w_ref, h_ref, out_ref,
                      o_tok_ref, l_tok_ref, mrg_ref):
    tm = h_ref.shape[0]
    slabs = D_MODEL // LANES
    o_in = (o0_ref, o1_ref, o2_ref)
    l_in = (l0_ref, l1_ref, l2_ref)
    for g, (_, dilation) in enumerate(DIL_GROUPS):
        for r in range(dilation):
            rows = slice(None) if dilation == 1 else pl.ds(r, tm // dilation, stride=dilation)
            l_tok_ref[g, rows, :] = l_in[g][:, r * LANES:(r + 1) * LANES]
            for c in range(slabs):
                lo = r * D_MODEL + c * LANES
                o_tok_ref[g * slabs + c, rows, :] = o_in[g][:, lo:lo + LANES]
    lses = [l_tok_ref[g] for g in range(len(DIL_GROUPS))]
    m = jnp.maximum(jnp.maximum(lses[0], lses[1]), lses[2])
    es = [jnp.exp(l - m) for l in lses]
    total = es[0] + es[1] + es[2]
    alphas = [e / total for e in es]
    for head in range(DIL_HEADS):
        merged = sum(alphas[g][:, head:head + 1] * o_tok_ref[g * slabs + head] for g in range(len(DIL_GROUPS)))
        mrg_ref[:, head * DIL_HEAD_DIM:(head + 1) * DIL_HEAD_DIM] = merged.astype(BF16)
    out_ref[...] = h_ref[...] + jnp.dot(mrg_ref[...], w_ref[...], preferred_element_type=F32)


def _dil_merge_out_proj(outs, lses, w, h, tm=512):
    t, d = h.shape
    n_groups = len(DIL_GROUPS)
    grouped = lambda cols, dil: pl.BlockSpec((tm // dil, dil * cols), lambda i: (i, 0))
    dils = [dil for _, dil in DIL_GROUPS]
    return pl.pallas_call(
        _dil_merge_kernel,
        grid=(t // tm,),
        in_specs=[grouped(d, dil) for dil in dils] + [grouped(LANES, dil) for dil in dils]
                 + [pl.BlockSpec(w.shape, lambda i: (0, 0)), pl.BlockSpec((tm, d), lambda i: (i, 0))],
        out_specs=pl.BlockSpec((tm, d), lambda i: (i, 0)),
        out_shape=jax.ShapeDtypeStruct((t, d), F32),
        scratch_shapes=[pltpu.VMEM((n_groups * d // LANES, tm, LANES), F32),
                        pltpu.VMEM((n_groups, tm, LANES), F32),
                        pltpu.VMEM((tm, d), BF16)],
        compiler_params=_params("parallel"),
        name="dil_merge_out_proj",
    )(*outs, *lses, w, h)


def _mla_proj_kernel(x_ref, g_ref, win_ref, qa_ref, kva_ref, wq_ref, wk_ref, wv_ref, qg_ref, kg_ref,
                     cos_ref, sin_up_ref, sin_dn_ref, ones_ref, q_out, k_out, v_out, qp_ref, kp_ref):
    xn = _rms(x_ref[...], g_ref[...]).astype(BF16)
    c = jnp.dot(xn, win_ref[...], preferred_element_type=F32)
    cq = _rms(c[:, :MLA_Q_RANK], qa_ref[...]).astype(BF16)
    ckv = _rms(c[:, MLA_Q_RANK:MLA_Q_RANK + MLA_KV_RANK], kva_ref[...]).astype(BF16)
    shared_rope = c[:, MLA_Q_RANK + MLA_KV_RANK:]
    qp_ref[...] = jnp.dot(cq, wq_ref[...], preferred_element_type=F32)
    kp_ref[...] = jnp.dot(ckv, wk_ref[...], preferred_element_type=F32)
    v_out[...] = jnp.dot(ckv, wv_ref[...], preferred_element_type=F32).astype(BF16)
    cos = cos_ref[...]
    sin_up = sin_up_ref[...]
    sin_dn = sin_dn_ref[...]
    half = MLA_ROPE // 2

    def norm_rope(x, gain):
        ms = _split_dot(x * x, ones_ref[...]) * (1.0 / MLA_QK)
        y = x * lax.rsqrt(ms + EPS) * gain
        return y * cos + pltpu.roll(y, half, 1) * sin_up + pltpu.roll(y, LANES - half, 1) * sin_dn

    for head in range(MLA_HEADS):
        sl = slice(head * LANES, (head + 1) * LANES)
        q_out[:, sl] = norm_rope(qp_ref[:, sl], qg_ref[...]).astype(BF16)
        k_out[:, sl] = norm_rope(kp_ref[:, sl] + shared_rope, kg_ref[...]).astype(BF16)


def _pad_heads(w, heads, src_lo, src_hi, src_width):
    k = w.shape[0]
    w = w.reshape(k, heads, src_width)[:, :, src_lo:src_hi]
    w = jnp.pad(w, ((0, 0), (0, 0), (0, LANES - (src_hi - src_lo))))
    return w.reshape(k, heads * LANES)


def _mla_proj(h, norm_g, w_in, q_a_gain, kv_a_gain, w_q_b, w_kv_b, q_gain, k_gain, seq, tm=512):
    t, d = h.shape
    latent = MLA_Q_RANK + MLA_KV_RANK
    w_in_p = jnp.concatenate([
        w_in[:, :latent],
        jnp.zeros((d, MLA_NOPE), w_in.dtype),
        w_in[:, latent:],
        jnp.zeros((d, LANES - MLA_QK), w_in.dtype)], axis=1).astype(BF16)
    wq = _pad_heads(w_q_b, MLA_HEADS, 0, MLA_QK, MLA_QK).astype(BF16)
    wk = _pad_heads(w_kv_b, MLA_HEADS, 0, MLA_NOPE, MLA_NOPE + MLA_V).astype(BF16)
    wv = w_kv_b.reshape(MLA_KV_RANK, MLA_HEADS, MLA_NOPE + MLA_V)[:, :, MLA_NOPE:]
    wv = wv.reshape(MLA_KV_RANK, MLA_HEADS * MLA_V).astype(BF16)
    pad = jnp.zeros((LANES - MLA_QK,), F32)
    qg = (jnp.concatenate([q_gain, pad]) * (LOG2_E / math.sqrt(MLA_QK))).reshape(1, LANES)
    kg = jnp.concatenate([k_gain, pad]).reshape(1, LANES)
    half = MLA_ROPE // 2
    inv = ROPE_THETA ** (-jnp.arange(half, dtype=F32) / half)
    ang = jnp.arange(seq, dtype=F32)[:, None] * inv[None, :]
    zeros = jnp.zeros((seq, half), F32)
    cos = jnp.concatenate([jnp.ones((seq, MLA_NOPE), F32), jnp.cos(ang), jnp.cos(ang),
                           jnp.zeros((seq, LANES - MLA_QK), F32)], axis=1)
    sin_up = jnp.concatenate([jnp.zeros((seq, MLA_NOPE), F32), zeros, jnp.sin(ang),
                              jnp.zeros((seq, LANES - MLA_QK), F32)], axis=1)
    sin_dn = jnp.concatenate([jnp.zeros((seq, MLA_NOPE), F32), -jnp.sin(ang), zeros,
                              jnp.zeros((seq, LANES - MLA_QK), F32)], axis=1)
    ones = jnp.ones((LANES, LANES), BF16)
    per_seq = seq // tm
    full = lambda a: pl.BlockSpec(a.shape, lambda i: (0,) * a.ndim)
    table = pl.BlockSpec((tm, LANES), lambda i: (i % per_seq, 0))
    consts = (norm_g.reshape(1, d), w_in_p, q_a_gain.reshape(1, -1), kv_a_gain.reshape(1, -1), wq, wk, wv, qg, kg)
    qk_cols = MLA_HEADS * LANES
    return pl.pallas_call(
        _mla_proj_kernel,
        grid=(t // tm,),
        in_specs=[pl.BlockSpec((tm, d), lambda i: (i, 0))] + [full(a) for a in consts]
                 + [table, table, table, full(ones)],
        out_specs=[
            pl.BlockSpec((tm, qk_cols), lambda i: (i, 0)),
            pl.BlockSpec((tm, qk_cols), lambda i: (i, 0)),
            pl.BlockSpec((tm, MLA_HEADS * MLA_V), lambda i: (i, 0)),
        ],
        out_shape=[
            jax.ShapeDtypeStruct((t, qk_cols), BF16),
            jax.ShapeDtypeStruct((t, qk_cols), BF16),
            jax.ShapeDtypeStruct((t, MLA_HEADS * MLA_V), BF16),
        ],
        scratch_shapes=[pltpu.VMEM((tm, qk_cols), F32), pltpu.VMEM((tm, qk_cols), F32)],
        compiler_params=_params("parallel"),
        name="mla_proj",
    )(h, *consts, cos, sin_up, sin_dn, ones)


def _mla_attn_kernel(q_ref, k_ref, v_ref, o_ref, *, bq, bk):
    i = pl.program_id(2)
    heads = LANES // MLA_V
    ratio = bq // bk
    lane = lax.broadcasted_iota(jnp.int32, (1, LANES), 1)
    row = lax.broadcasted_iota(jnp.int32, (bq, bk), 0)
    col = lax.broadcasted_iota(jnp.int32, (bq, bk), 1)

    def tile(j, state, diagonal):
        start = pl.multiple_of(j * bk, bk)
        vals = v_ref[pl.ds(start, bk), :]
        lanes = [slice(head * LANES, (head + 1) * LANES) for head in range(heads)]
        ss = [_dot_t(q_ref[:, sl], k_ref[pl.ds(start, bk), sl]) for sl in lanes]
        if diagonal:
            causal = col + (j * bk - i * bq) <= row
            ss = [jnp.where(causal, s, MASKED) for s in ss]
        ms = [jnp.maximum(state[head][0], jnp.max(ss[head], axis=-1, keepdims=True)) for head in range(heads)]
        ps = [jnp.exp2(ss[head] - ms[head]) for head in range(heads)]
        pvs = [jnp.dot(p.astype(BF16), vals, preferred_element_type=F32) for p in ps]
        new_state = []
        for head in range(heads):
            m, l, acc = state[head]
            corr = jnp.exp2(m - ms[head])
            new_state.append((ms[head], corr * l + jnp.sum(ps[head], axis=-1, keepdims=True),
                              corr * acc + pvs[head]))
        return tuple(new_state)

    state = tuple((jnp.full((bq, 1), MASKED, F32), jnp.zeros((bq, 1), F32), jnp.zeros((bq, LANES), F32))
                  for _ in range(heads))
    for d in range(ratio):
        state = tile(ratio * i + d, state, True)
    state = lax.fori_loop(0, ratio * i, lambda t, s: tile(ratio * i - 1 - t, s, False), state)
    out = sum(jnp.where((lane // MLA_V) == head, state[head][2] / state[head][1], 0.0) for head in range(heads))
    o_ref[...] = out.astype(BF16)


def _mla_attention(q, k, v, batch, seq, bq=512, bk=256):
    pairs = MLA_HEADS * MLA_V // LANES
    q3 = q.reshape(batch, seq, -1)
    k3 = k.reshape(batch, seq, -1)
    v3 = v.reshape(batch, seq, -1)
    out = pl.pallas_call(
        functools.partial(_mla_attn_kernel, bq=bq, bk=bk),
        grid=(batch, pairs, seq // bq),
        in_specs=[
            pl.BlockSpec((None, bq, 2 * LANES), lambda b, p, i: (b, i, p)),
            pl.BlockSpec((None, seq, 2 * LANES), lambda b, p, i: (b, 0, p)),
            pl.BlockSpec((None, seq, LANES), lambda b, p, i: (b, 0, p)),
        ],
        out_specs=pl.BlockSpec((None, bq, LANES), lambda b, p, i: (b, i, p)),
        out_shape=jax.ShapeDtypeStruct((batch, seq, MLA_HEADS * MLA_V), BF16),
        compiler_params=_params("parallel", "parallel", "arbitrary"),
        name="mla_attention",
    )(q3, k3, v3)
    return out.reshape(batch * seq, MLA_HEADS * MLA_V)


def _sb_layer(h, batch, seq, mix_norm, w_in, q_norm, k_norm, w_out):
    scale = LOG2_E / math.sqrt(SB_HEAD_DIM)
    gains = jnp.stack([jnp.tile(q_norm, SB_HEADS) * scale, jnp.tile(k_norm, SB_HEADS)])
    qkv = _proj_qkv(h, mix_norm, w_in.astype(BF16), gains, SB_HEAD_DIM)
    o = _sb_attention(qkv, batch, seq)
    return _out_proj(o, w_out.astype(BF16), h)


def _dil_layer(h, batch, seq, mix_norm, w_in, q_norm, k_norm, w_out):
    scale = DIL_HEAD_DIM ** -0.5
    w_in = w_in.astype(BF16)
    cols = 3 * D_MODEL
    outs, lses = [], []
    for g, (_, dilation) in enumerate(DIL_GROUPS):
        gains = jnp.stack([jnp.tile(q_norm[g], DIL_HEADS) * scale, jnp.tile(k_norm[g], DIL_HEADS)])
        qkv = _proj_qkv(h, mix_norm, w_in[:, g * cols:(g + 1) * cols], gains, DIL_HEAD_DIM, dilation)
        o, lse = _dil_attention(qkv, g, batch, seq)
        outs.append(o)
        lses.append(lse)
    return _dil_merge_out_proj(outs, lses, w_out.astype(BF16), h)


def _mla_layer(h, batch, seq, mix_norm, w_in, q_a_norm, kv_a_norm, w_q_b, w_kv_b, q_norm, k_norm, w_out):
    q, k, v = _mla_proj(h, mix_norm, w_in, q_a_norm, kv_a_norm, w_q_b, w_kv_b, q_norm, k_norm, seq)
    o = _mla_attention(q, k, v, batch, seq)
    return _out_proj(o, w_out.astype(BF16), h)


def _trunk(x, layers):
    batch, seq, d = x.shape
    h = x.reshape(batch * seq, d)
    mixers = (_sb_layer, _dil_layer, _mla_layer)
    for idx, (mix_norm, mix_params, mlp_norm, w_up, w_down) in enumerate(layers):
        h = mixers[idx % len(mixers)](h, batch, seq, mix_norm, *mix_params)
        h = _mlp(h, mlp_norm, w_up.astype(BF16), w_down.astype(BF16))
    return h.reshape(batch, seq, d)


def kernel(x, l0_mix_norm, l0_sb_w_in, l0_sb_q_norm, l0_sb_k_norm, l0_sb_w_out, l0_mlp_norm, l0_mlp_w_up, l0_mlp_w_down, l1_mix_norm, l1_dil_w_in, l1_dil_q_norm, l1_dil_k_norm, l1_dil_w_out, l1_mlp_norm, l1_mlp_w_up, l1_mlp_w_down, l2_mix_norm, l2_mla_w_in, l2_mla_q_a_norm, l2_mla_kv_a_norm, l2_mla_w_q_b, l2_mla_w_kv_b, l2_mla_q_norm, l2_mla_k_norm, l2_mla_w_out, l2_mlp_norm, l2_mlp_w_up, l2_mlp_w_down, l3_mix_norm, l3_sb_w_in, l3_sb_q_norm, l3_sb_k_norm, l3_sb_w_out, l3_mlp_norm, l3_mlp_w_up, l3_mlp_w_down):
    layers = [
        (l0_mix_norm, (l0_sb_w_in, l0_sb_q_norm, l0_sb_k_norm, l0_sb_w_out),
         l0_mlp_norm, l0_mlp_w_up, l0_mlp_w_down),
        (l1_mix_norm, (l1_dil_w_in, l1_dil_q_norm, l1_dil_k_norm, l1_dil_w_out),
         l1_mlp_norm, l1_mlp_w_up, l1_mlp_w_down),
        (l2_mix_norm, (l2_mla_w_in, l2_mla_q_a_norm, l2_mla_kv_a_norm, l2_mla_w_q_b,
                       l2_mla_w_kv_b, l2_mla_q_norm, l2_mla_k_norm, l2_mla_w_out),
         l2_mlp_norm, l2_mlp_w_up, l2_mlp_w_down),
        (l3_mix_norm, (l3_sb_w_in, l3_sb_q_norm, l3_sb_k_norm, l3_sb_w_out),
         l3_mlp_norm, l3_mlp_w_up, l3_mlp_w_down),
    ]
    return _trunk(x, layers)
```

```python
import functools
import math

import jax
import jax.numpy as jnp
from jax import lax
from jax.experimental import pallas as pl
from jax.experimental.pallas import tpu as pltpu

F32 = jnp.float32
BF16 = jnp.bfloat16

EPS = 1e-6
LANES = 128
MXU_WIDTH = 256
MASKED = -1e30
LOG2_E = math.log2(math.e)
SB_SATURATED = 150.0
VMEM_LIMIT_BYTES = 48 * 1024 * 1024

D_MODEL = 1024
SB_HEADS = 16
SB_HEAD_DIM = 64
DIL_GROUPS = ((128, 1), (512, 4), (2048, 16))
DIL_HEADS = 8
DIL_HEAD_DIM = 128
DIL_BLOCK = 128
MLA_HEADS = 16
MLA_NOPE = 64
MLA_ROPE = 32
MLA_V = 64
MLA_QK = MLA_NOPE + MLA_ROPE
MLA_Q_RANK = 384
MLA_KV_RANK = 256
ROPE_THETA = 10000.0


def _params(*semantics):
    return pltpu.CompilerParams(dimension_semantics=semantics, vmem_limit_bytes=VMEM_LIMIT_BYTES)


def _rms(x, g):
    ms = jnp.mean(x * x, axis=-1, keepdims=True)
    return x * lax.rsqrt(ms + EPS) * g


def _split_dot(x, w):
    hi = x.astype(BF16)
    lo = (x - hi.astype(F32)).astype(BF16)
    return (jnp.dot(hi, w, preferred_element_type=F32)
            + jnp.dot(lo, w, preferred_element_type=F32))


def _dot_t(a, b):
    return lax.dot_general(a, b, (((1,), (1,)), ((), ())), preferred_element_type=F32)


def _proj_qkv_kernel(x_ref, g_ref, w_ref, gain_ref, seg_ref, o_ref, xn_ref, *y_scratch, head_dim, dilation):
    j = pl.program_id(1)
    tm, width = xn_ref.shape[0], w_ref.shape[1]
    slabs = width // LANES
    seg_width = seg_ref.shape[0]

    @pl.when(j == 0)
    def _():
        xn_ref[...] = _rms(x_ref[...], g_ref[...]).astype(BF16)

    def project(lo):
        return jnp.dot(xn_ref[...], w_ref[:, lo:lo + seg_width], preferred_element_type=F32)

    def finish(block, lo, yc):
        if block < 2:
            ms = _split_dot(yc * yc, seg_ref[...]) * (1.0 / head_dim)
            yc = yc * lax.rsqrt(ms + EPS) * gain_ref[:, lo:lo + seg_width]
        for c in range(seg_width // LANES):
            piece = yc[:, c * LANES:(c + 1) * LANES]
            if dilation == 1:
                o_ref[:, block * width + lo + c * LANES:block * width + lo + (c + 1) * LANES] = piece.astype(BF16)
            else:
                y_scratch[0][lo // LANES + c] = piece

    def emit(block):
        chunks = list(range(0, width, seg_width))
        pending = project(chunks[0])
        for nxt in chunks[1:]:
            following = project(nxt)
            finish(block, nxt - seg_width, pending)
            pending = following
        finish(block, chunks[-1], pending)
        if dilation > 1:
            for r in range(dilation):
                for c in range(slabs):
                    rows = y_scratch[0][c, pl.ds(r, tm // dilation, stride=dilation), :]
                    lo = (r * 3 + block) * width + c * LANES
                    o_ref[:, lo:lo + LANES] = rows.astype(BF16)

    for block in range(3):
        pl.when(j == block)(functools.partial(emit, block))


def _proj_qkv(h, norm_g, w, gains, head_dim, dilation=1, tm=512):
    t, d = h.shape
    width = D_MODEL
    assert w.shape == (d, 3 * width) and tm % (16 * dilation) == 0
    lane = jnp.arange(MXU_WIDTH)
    seg = (lane[:, None] // head_dim == lane[None, :] // head_dim).astype(BF16)
    return pl.pallas_call(
        functools.partial(_proj_qkv_kernel, head_dim=head_dim, dilation=dilation),
        grid=(t // tm, 3),
        in_specs=[
            pl.BlockSpec((tm, d), lambda i, j: (i, 0)),
            pl.BlockSpec((1, d), lambda i, j: (0, 0)),
            pl.BlockSpec((d, width), lambda i, j: (0, j)),
            pl.BlockSpec((None, 1, width), lambda i, j: (jnp.minimum(j, 1), 0, 0)),
            pl.BlockSpec((MXU_WIDTH, MXU_WIDTH), lambda i, j: (0, 0)),
        ],
        out_specs=pl.BlockSpec((tm // dilation, dilation * 3 * width), lambda i, j: (i, 0)),
        out_shape=jax.ShapeDtypeStruct((t // dilation, dilation * 3 * width), BF16),
        scratch_shapes=[pltpu.VMEM((tm, d), BF16)]
                       + ([pltpu.VMEM((width // LANES, tm, LANES), F32)] if dilation > 1 else []),
        compiler_params=_params("parallel", "arbitrary"),
        name="proj_qkv_d%d" % dilation,
    )(h, norm_g.reshape(1, d), w, gains.reshape(2, 1, width), seg)


def _sb_attn_kernel(q_ref, k_ref, v_ref, tri_ref, o_ref, *, blk):
    i = pl.program_id(2)
    lane = lax.broadcasted_iota(jnp.int32, (1, LANES), 1)
    row = lax.broadcasted_iota(jnp.int32, (blk, blk), 0)
    col = lax.broadcasted_iota(jnp.int32, (blk, blk), 1)
    strictly_past = col < row
    q = q_ref[...]
    heads = LANES // SB_HEAD_DIM
    head_lanes = [(lane // SB_HEAD_DIM) == head for head in range(heads)]
    qs = [jnp.where(hl, q, jnp.zeros_like(q)) for hl in head_lanes]

    def tiles(js, masks, state):
        starts = [pl.multiple_of(jnp.maximum(j, 0) * blk, blk) for j in js]
        keys = [k_ref[pl.ds(start, blk), :] for start in starts]
        vals = [v_ref[pl.ds(start, blk), :] for start in starts]
        pairs = [(t, head) for t in range(len(js)) for head in range(heads)]
        zs = {}
        for t, head in pairs:
            z = _dot_t(qs[head], keys[t])
            zs[t, head] = z if masks[t] is None else jnp.where(masks[t], z, MASKED)
        sps = {p: jnp.maximum(zs[p], 0.0) + jnp.log2(1.0 + jnp.exp2(-jnp.abs(zs[p]))) for p in pairs}
        withins = {p: _split_dot(sps[p], tri_ref[...]) for p in pairs}
        later = [state[head][1] for head in range(heads)]
        probs = {}
        for t, head in pairs:
            probs[t, head] = jnp.exp2(zs[t, head] - withins[t, head] - later[head]).astype(BF16)
            later[head] = later[head] + withins[t, head][:, 0:1]
        acc = [state[head][0] for head in range(heads)]
        for t, head in pairs:
            acc[head] = acc[head] + jnp.dot(probs[t, head], vals[t], preferred_element_type=F32)
        return tuple((acc[head], later[head]) for head in range(heads))

    state = tuple((jnp.zeros((blk, LANES), F32), jnp.zeros((blk, 1), F32)) for _ in range(heads))
    state = tiles([i, i - 1], [strictly_past, i > 0], state)
    n_pairs = lax.shift_right_logical(jnp.maximum(i - 1, 0) + 1, 1)

    def unsaturated(s):
        lowest = jnp.min(jnp.minimum(s[0][1], s[1][1]))
        return (lowest < SB_SATURATED).astype(jnp.int32)

    def pair(carry):
        t, _, s = carry
        j = i - 2 - 2 * t
        s = tiles([j, j - 1], [None, j > 0], s)
        return t + 1, unsaturated(s), s

    _, _, state = lax.while_loop(lambda c: jnp.logical_and(c[0] < n_pairs, c[1] > 0), pair,
                                 (jnp.int32(0), unsaturated(state), state))
    out = sum(jnp.where(head_lanes[head], state[head][0], 0.0) for head in range(heads))
    o_ref[...] = out.astype(BF16)


def _sb_attention(qkv, batch, seq, blk=256):
    pairs = D_MODEL // LANES
    nq = seq // blk
    qkv3 = qkv.reshape(batch, seq, 3 * D_MODEL)
    idx = jnp.arange(blk)
    tri = (idx[:, None] >= idx[None, :]).astype(BF16)
    out = pl.pallas_call(
        functools.partial(_sb_attn_kernel, blk=blk),
        grid=(batch, pairs, nq),
        in_specs=[
            pl.BlockSpec((None, blk, LANES), lambda b, p, i: (b, i, p)),
            pl.BlockSpec((None, seq, LANES), lambda b, p, i: (b, 0, pairs + p)),
            pl.BlockSpec((None, seq, LANES), lambda b, p, i: (b, 0, 2 * pairs + p)),
            pl.BlockSpec((blk, blk), lambda b, p, i: (0, 0)),
        ],
        out_specs=pl.BlockSpec((None, blk, LANES), lambda b, p, i: (b, i, p)),
        out_shape=jax.ShapeDtypeStruct((batch, seq, D_MODEL), BF16),
        compiler_params=_params("parallel", "parallel", "arbitrary"),
        name="sb_attention",
    )(qkv3, qkv3, qkv3, tri)
    return out.reshape(batch * seq, D_MODEL)


def _out_proj_kernel(o_ref, w_ref, h_ref, out_ref):
    out_ref[...] = h_ref[...] + jnp.dot(o_ref[...], w_ref[...], preferred_element_type=F32)


def _out_proj(o, w, h, tm=1024):
    t, d = h.shape
    return pl.pallas_call(
        _out_proj_kernel,
        grid=(t // tm,),
        in_specs=[
            pl.BlockSpec((tm, o.shape[1]), lambda i: (i, 0)),
            pl.BlockSpec(w.shape, lambda i: (0, 0)),
            pl.BlockSpec((tm, d), lambda i: (i, 0)),
        ],
        out_specs=pl.BlockSpec((tm, d), lambda i: (i, 0)),
        out_shape=jax.ShapeDtypeStruct((t, d), F32),
        compiler_params=_params("parallel"),
        name="out_proj",
    )(o, w, h)


def _mlp_kernel(h_ref, g_ref, wu_ref, wd_ref, out_ref, xn_ref, acc_ref):
    f = pl.program_id(1)

    @pl.when(f == 0)
    def _():
        xn_ref[...] = _rms(h_ref[...], g_ref[...]).astype(BF16)
        acc_ref[...] = jnp.zeros_like(acc_ref)

    u = jnp.maximum(jnp.dot(xn_ref[...], wu_ref[...], preferred_element_type=F32), 0.0)
    acc_ref[...] += jnp.dot((u * u).astype(BF16), wd_ref[...], preferred_element_type=F32)

    @pl.when(f == pl.num_programs(1) - 1)
    def _():
        out_ref[...] = h_ref[...] + acc_ref[...]


def _mlp(h, norm_g, w_up, w_down, tm=1024, fc=512):
    t, d = h.shape
    ff = w_up.shape[1]
    return pl.pallas_call(
        _mlp_kernel,
        grid=(t // tm, ff // fc),
        in_specs=[
            pl.BlockSpec((tm, d), lambda i, f: (i, 0)),
            pl.BlockSpec((1, d), lambda i, f: (0, 0)),
            pl.BlockSpec((d, fc), lambda i, f: (0, f)),
            pl.BlockSpec((fc, d), lambda i, f: (f, 0)),
        ],
        out_specs=pl.BlockSpec((tm, d), lambda i, f: (i, 0)),
        out_shape=jax.ShapeDtypeStruct((t, d), F32),
        scratch_shapes=[pltpu.VMEM((tm, d), BF16), pltpu.VMEM((tm, d), F32)],
        compiler_params=_params("parallel", "arbitrary"),
        name="mlp",
    )(h, norm_g.reshape(1, d), w_up, w_down)


def _dil_attn_kernel(q_ref, kc_ref, kp_ref, vc_ref, vp_ref, o_ref, lse_ref, *, bias_per_step, max_steps):
    n = pl.program_id(2)
    blk = DIL_BLOCK
    qi = lax.broadcasted_iota(jnp.int32, (blk, 2 * blk), 0)
    kj = lax.broadcasted_iota(jnp.int32, (blk, 2 * blk), 1)
    steps = blk + qi - kj
    valid = (steps >= 0) & (steps <= max_steps) & ((kj >= blk) | (n > 0))
    steps_f = steps.astype(F32)
    lane = lax.broadcasted_iota(jnp.int32, (1, LANES), 1)
    lse_tile = jnp.zeros((blk, LANES), F32)
    for head in range(DIL_HEADS):
        sl = slice(head * DIL_HEAD_DIM, (head + 1) * DIL_HEAD_DIM)
        keys = jnp.concatenate([kp_ref[:, sl], kc_ref[:, sl]], axis=0)
        vals = jnp.concatenate([vp_ref[:, sl], vc_ref[:, sl]], axis=0)
        z = _dot_t(q_ref[:, sl], keys)
        z = jnp.where(valid, z - bias_per_step[head] * steps_f, MASKED)
        m = jnp.max(z, axis=-1, keepdims=True)
        p = jnp.exp(z - m)
        denom = jnp.sum(p, axis=-1, keepdims=True)
        o_ref[:, sl] = jnp.dot(p.astype(BF16), vals, preferred_element_type=F32) / denom
        lse_tile = jnp.where(lane == head, m + jnp.log(denom), lse_tile)
    lse_ref[...] = lse_tile


def _dil_attention(qkv, group, batch, seq):
    window, dilation = DIL_GROUPS[group]
    n_groups = len(DIL_GROUPS)
    blk = DIL_BLOCK
    length = seq // dilation
    assert length % blk == 0
    nb = length // blk
    width = DIL_HEADS * DIL_HEAD_DIM
    slopes = [2.0 ** (-8.0 * (group * DIL_HEADS + hd + 1) / (n_groups * DIL_HEADS)) for hd in range(DIL_HEADS)]
    view = qkv.reshape(batch, length, dilation * 3 * width)

    def col(offset):
        return lambda b, r, n: (b, n, r * 3 + offset)

    def col_prev(offset):
        return lambda b, r, n: (b, jnp.maximum(n - 1, 0), r * 3 + offset)

    o, lse = pl.pallas_call(
        functools.partial(_dil_attn_kernel,
                          bias_per_step=tuple(s * dilation for s in slopes),
                          max_steps=window // dilation),
        grid=(batch, dilation, nb),
        in_specs=[
            pl.BlockSpec((None, blk, width), col(0)),
            pl.BlockSpec((None, blk, width), col(1)),
            pl.BlockSpec((None, blk, width), col_prev(1)),
            pl.BlockSpec((None, blk, width), col(2)),
            pl.BlockSpec((None, blk, width), col_prev(2)),
        ],
        out_specs=[
            pl.BlockSpec((None, blk, width), lambda b, r, n: (b, n, r)),
            pl.BlockSpec((None, blk, LANES), lambda b, r, n: (b, n, r)),
        ],
        out_shape=[
            jax.ShapeDtypeStruct((batch, length, dilation * width), F32),
            jax.ShapeDtypeStruct((batch, length, dilation * LANES), F32),
        ],
        compiler_params=_params("parallel", "parallel", "arbitrary"),
        name="dil_attention_g%d" % group,
    )(view, view, view, view, view)
    rows = batch * length
    return o.reshape(rows, dilation * width), lse.reshape(rows, dilation * LANES)


def _dil_merge_kernel(o0_ref, o1_ref, o2_ref, l0_ref, l1_ref, l2_ref, w_ref, h_ref, out_ref,
                      o_tok_ref, l_tok_ref, mrg_ref):
    tm = h_ref.shape[0]
    slabs = D_MODEL // LANES
    o_in = (o0_ref, o1_ref, o2_ref)
    l_in = (l0_ref, l1_ref, l2_ref)
    for g, (_, dilation) in enumerate(DIL_GROUPS):
        for r in range(dilation):
            rows = slice(None) if dilation == 1 else pl.ds(r, tm // dilation, stride=dilation)
            l_tok_ref[g, rows, :] = l_in[g][:, r * LANES:(r + 1) * LANES]
            for c in range(slabs):
                lo = r * D_MODEL + c * LANES
                o_tok_ref[g * slabs + c, rows, :] = o_in[g][:, lo:lo + LANES]
    lses = [l_tok_ref[g] for g in range(len(DIL_GROUPS))]
    m = jnp.maximum(jnp.maximum(lses[0], lses[1]), lses[2])
    es = [jnp.exp(l - m) for l in lses]
    total = es[0] + es[1] + es[2]
    alphas = [e / total for e in es]
    for head in range(DIL_HEADS):
        merged = sum(alphas[g][:, head:head + 1] * o_tok_ref[g * slabs + head] for g in range(len(DIL_GROUPS)))
        mrg_ref[:, head * DIL_HEAD_DIM:(head + 1) * DIL_HEAD_DIM] = merged.astype(BF16)
    out_ref[...] = h_ref[...] + jnp.dot(mrg_ref[...], w_ref[...], preferred_element_type=F32)


def _dil_merge_out_proj(outs, lses, w, h, tm=512):
    t, d = h.shape
    n_groups = len(DIL_GROUPS)
    grouped = lambda cols, dil: pl.BlockSpec((tm // dil, dil * cols), lambda i: (i, 0))
    dils = [dil for _, dil in DIL_GROUPS]
    return pl.pallas_call(
        _dil_merge_kernel,
        grid=(t // tm,),
        in_specs=[grouped(d, dil) for dil in dils] + [grouped(LANES, dil) for dil in dils]
                 + [pl.BlockSpec(w.shape, lambda i: (0, 0)), pl.BlockSpec((tm, d), lambda i: (i, 0))],
        out_specs=pl.BlockSpec((tm, d), lambda i: (i, 0)),
        out_shape=jax.ShapeDtypeStruct((t, d), F32),
        scratch_shapes=[pltpu.VMEM((n_groups * d // LANES, tm, LANES), F32),
                        pltpu.VMEM((n_groups, tm, LANES), F32),
                        pltpu.VMEM((tm, d), BF16)],
        compiler_params=_params("parallel"),
        name="dil_merge_out_proj",
    )(*outs, *lses, w, h)


def _mla_proj_kernel(x_ref, g_ref, win_ref, qa_ref, kva_ref, wq_ref, wk_ref, wv_ref, qg_ref, kg_ref,
                     cos_ref, sin_up_ref, sin_dn_ref, ones_ref, q_out, k_out, v_out, qp_ref, kp_ref):
    xn = _rms(x_ref[...], g_ref[...]).astype(BF16)
    c = jnp.dot(xn, win_ref[...], preferred_element_type=F32)
    cq = _rms(c[:, :MLA_Q_RANK], qa_ref[...]).astype(BF16)
    ckv = _rms(c[:, MLA_Q_RANK:MLA_Q_RANK + MLA_KV_RANK], kva_ref[...]).astype(BF16)
    shared_rope = c[:, MLA_Q_RANK + MLA_KV_RANK:]
    qp_ref[...] = jnp.dot(cq, wq_ref[...], preferred_element_type=F32)
    kp_ref[...] = jnp.dot(ckv, wk_ref[...], preferred_element_type=F32)
    v_out[...] = jnp.dot(ckv, wv_ref[...], preferred_element_type=F32).astype(BF16)
    cos = cos_ref[...]
    sin_up = sin_up_ref[...]
    sin_dn = sin_dn_ref[...]
    half = MLA_ROPE // 2

    def norm_rope(x, gain):
        ms = _split_dot(x * x, ones_ref[...]) * (1.0 / MLA_QK)
        y = x * lax.rsqrt(ms + EPS) * gain
        return y * cos + pltpu.roll(y, half, 1) * sin_up + pltpu.roll(y, LANES - half, 1) * sin_dn

    for head in range(MLA_HEADS):
        sl = slice(head * LANES, (head + 1) * LANES)
        q_out[:, sl] = norm_rope(qp_ref[:, sl], qg_ref[...]).astype(BF16)
        k_out[:, sl] = norm_rope(kp_ref[:, sl] + shared_rope, kg_ref[...]).astype(BF16)


def _pad_heads(w, heads, src_lo, src_hi, src_width):
    k = w.shape[0]
    w = w.reshape(k, heads, src_width)[:, :, src_lo:src_hi]
    w = jnp.pad(w, ((0, 0), (0, 0), (0, LANES - (src_hi - src_lo))))
    return w.reshape(k, heads * LANES)


def _mla_proj(h, norm_g, w_in, q_a_gain, kv_a_gain, w_q_b, w_kv_b, q_gain, k_gain, seq, tm=512):
    t, d = h.shape
    latent = MLA_Q_RANK + MLA_KV_RANK
    w_in_p = jnp.concatenate([
        w_in[:, :latent],
        jnp.zeros((d, MLA_NOPE), w_in.dtype),
        w_in[:, latent:],
        jnp.zeros((d, LANES - MLA_QK), w_in.dtype)], axis=1).astype(BF16)
    wq = _pad_heads(w_q_b, MLA_HEADS, 0, MLA_QK, MLA_QK).astype(BF16)
    wk = _pad_heads(w_kv_b, MLA_HEADS, 0, MLA_NOPE, MLA_NOPE + MLA_V).astype(BF16)
    wv = w_kv_b.reshape(MLA_KV_RANK, MLA_HEADS, MLA_NOPE + MLA_V)[:, :, MLA_NOPE:]
    wv = wv.reshape(MLA_KV_RANK, MLA_HEADS * MLA_V).astype(BF16)
    pad = jnp.zeros((LANES - MLA_QK,), F32)
    qg = (jnp.concatenate([q_gain, pad]) * (LOG2_E / math.sqrt(MLA_QK))).reshape(1, LANES)
    kg = jnp.concatenate([k_gain, pad]).reshape(1, LANES)
    half = MLA_ROPE // 2
    inv = ROPE_THETA ** (-jnp.arange(half, dtype=F32) / half)
    ang = jnp.arange(seq, dtype=F32)[:, None] * inv[None, :]
    zeros = jnp.zeros((seq, half), F32)
    cos = jnp.concatenate([jnp.ones((seq, MLA_NOPE), F32), jnp.cos(ang), jnp.cos(ang),
                           jnp.zeros((seq, LANES - MLA_QK), F32)], axis=1)
    sin_up = jnp.concatenate([jnp.zeros((seq, MLA_NOPE), F32), zeros, jnp.sin(ang),
                              jnp.zeros((seq, LANES - MLA_QK), F32)], axis=1)
    sin_dn = jnp.concatenate([jnp.zeros((seq, MLA_NOPE), F32), -jnp.sin(ang), zeros,
                              jnp.zeros((seq, LANES - MLA_QK), F32)], axis=1)
    ones = jnp.ones((LANES, LANES), BF16)
    per_seq = seq // tm
    full = lambda a: pl.BlockSpec(a.shape, lambda i: (0,) * a.ndim)
    table = pl.BlockSpec((tm, LANES), lambda i: (i % per_seq, 0))
    consts = (norm_g.reshape(1, d), w_in_p, q_a_gain.reshape(1, -1), kv_a_gain.reshape(1, -1), wq, wk, wv, qg, kg)
    qk_cols = MLA_HEADS * LANES
    return pl.pallas_call(
        _mla_proj_kernel,
        grid=(t // tm,),
        in_specs=[pl.BlockSpec((tm, d), lambda i: (i, 0))] + [full(a) for a in consts]
                 + [table, table, table, full(ones)],
        out_specs=[
            pl.BlockSpec((tm, qk_cols), lambda i: (i, 0)),
            pl.BlockSpec((tm, qk_cols), lambda i: (i, 0)),
            pl.BlockSpec((tm, MLA_HEADS * MLA_V), lambda i: (i, 0)),
        ],
        out_shape=[
            jax.ShapeDtypeStruct((t, qk_cols), BF16),
            jax.ShapeDtypeStruct((t, qk_cols), BF16),
            jax.ShapeDtypeStruct((t, MLA_HEADS * MLA_V), BF16),
        ],
        scratch_shapes=[pltpu.VMEM((tm, qk_cols), F32), pltpu.VMEM((tm, qk_cols), F32)],
        compiler_params=_params("parallel"),
        name="mla_proj",
    )(h, *consts, cos, sin_up, sin_dn, ones)


def _mla_attn_kernel(q_ref, k_ref, v_ref, o_ref, *, bq, bk):
    i = pl.program_id(2)
    heads = LANES // MLA_V
    ratio = bq // bk
    lane = lax.broadcasted_iota(jnp.int32, (1, LANES), 1)
    row = lax.broadcasted_iota(jnp.int32, (bq, bk), 0)
    col = lax.broadcasted_iota(jnp.int32, (bq, bk), 1)

    def tile(j, state, diagonal):
        start = pl.multiple_of(j * bk, bk)
        vals = v_ref[pl.ds(start, bk), :]
        lanes = [slice(head * LANES, (head + 1) * LANES) for head in range(heads)]
        ss = [_dot_t(q_ref[:, sl], k_ref[pl.ds(start, bk), sl]) for sl in lanes]
        if diagonal:
            causal = col + (j * bk - i * bq) <= row
            ss = [jnp.where(causal, s, MASKED) for s in ss]
        ms = [jnp.maximum(state[head][0], jnp.max(ss[head], axis=-1, keepdims=True)) for head in range(heads)]
        ps = [jnp.exp2(ss[head] - ms[head]) for head in range(heads)]
        pvs = [jnp.dot(p.astype(BF16), vals, preferred_element_type=F32) for p in ps]
        new_state = []
        for head in range(heads):
            m, l, acc = state[head]
            corr = jnp.exp2(m - ms[head])
            new_state.append((ms[head], corr * l + jnp.sum(ps[head], axis=-1, keepdims=True),
                              corr * acc + pvs[head]))
        return tuple(new_state)

    state = tuple((jnp.full((bq, 1), MASKED, F32), jnp.zeros((bq, 1), F32), jnp.zeros((bq, LANES), F32))
                  for _ in range(heads))
    for d in range(ratio):
        state = tile(ratio * i + d, state, True)
    state = lax.fori_loop(0, ratio * i, lambda t, s: tile(ratio * i - 1 - t, s, False), state)
    out = sum(jnp.where((lane // MLA_V) == head, state[head][2] / state[head][1], 0.0) for head in range(heads))
    o_ref[...] = out.astype(BF16)


def _mla_attention(q, k, v, batch, seq, bq=512, bk=256):
    pairs = MLA_HEADS * MLA_V // LANES
    q3 = q.reshape(batch, seq, -1)
    k3 = k.reshape(batch, seq, -1)
    v3 = v.reshape(batch, seq, -1)
    out = pl.pallas_call(
        functools.partial(_mla_attn_kernel, bq=bq, bk=bk),
        grid=(batch, pairs, seq // bq),
        in_specs=[
            pl.BlockSpec((None, bq, 2 * LANES), lambda b, p, i: (b, i, p)),
            pl.BlockSpec((None, seq, 2 * LANES), lambda b, p, i: (b, 0, p)),
            pl.BlockSpec((None, seq, LANES), lambda b, p, i: (b, 0, p)),
        ],
        out_specs=pl.BlockSpec((None, bq, LANES), lambda b, p, i: (b, i, p)),
        out_shape=jax.ShapeDtypeStruct((batch, seq, MLA_HEADS * MLA_V), BF16),
        compiler_params=_params("parallel", "parallel", "arbitrary"),
        name="mla_attention",
    )(q3, k3, v3)
    return out.reshape(batch * seq, MLA_HEADS * MLA_V)


def _sb_layer(h, batch, seq, mix_norm, w_in, q_norm, k_norm, w_out):
    scale = LOG2_E / math.sqrt(SB_HEAD_DIM)
    gains = jnp.stack([jnp.tile(q_norm, SB_HEADS) * scale, jnp.tile(k_norm, SB_HEADS)])
    qkv = _proj_qkv(h, mix_norm, w_in.astype(BF16), gains, SB_HEAD_DIM)
    o = _sb_attention(qkv, batch, seq)
    return _out_proj(o, w_out.astype(BF16), h)


def _dil_layer(h, batch, seq, mix_norm, w_in, q_norm, k_norm, w_out):
    scale = DIL_HEAD_DIM ** -0.5
    w_in = w_in.astype(BF16)
    cols = 3 * D_MODEL
    outs, lses = [], []
    for g, (_, dilation) in enumerate(DIL_GROUPS):
        gains = jnp.stack([jnp.tile(q_norm[g], DIL_HEADS) * scale, jnp.tile(k_norm[g], DIL_HEADS)])
        qkv = _proj_qkv(h, mix_norm, w_in[:, g * cols:(g + 1) * cols], gains, DIL_HEAD_DIM, dilation)
        o, lse = _dil_attention(qkv, g, batch, seq)
        outs.append(o)
        lses.append(lse)
    return _dil_merge_out_proj(outs, lses, w_out.astype(BF16), h)


def _mla_layer(h, batch, seq, mix_norm, w_in, q_a_norm, kv_a_norm, w_q_b, w_kv_b, q_norm, k_norm, w_out):
    q, k, v = _mla_proj(h, mix_norm, w_in, q_a_norm, kv_a_norm, w_q_b, w_kv_b, q_norm, k_norm, seq)
    o = _mla_attention(q, k, v, batch, seq)
    return _out_proj(o, w_out.astype(BF16), h)


def _trunk(x, layers):
    batch, seq, d = x.shape
    h = x.reshape(batch * seq, d)
    mixers = (_sb_layer, _dil_layer, _mla_layer)
    for idx, (mix_norm, mix_params, mlp_norm, w_up, w_down) in enumerate(layers):
        h = mixers[idx % len(mixers)](h, batch, seq, mix_norm, *mix_params)
        h = _mlp(h, mlp_norm, w_up.astype(BF16), w_down.astype(BF16))
    return h.reshape(batch, seq, d)


def kernel(x, l0_mix_norm, l0_sb_w_in, l0_sb_q_norm, l0_sb_k_norm, l0_sb_w_out, l0_mlp_norm, l0_mlp_w_up, l0_mlp_w_down, l1_mix_norm, l1_dil_w_in, l1_dil_q_norm, l1_dil_k_norm, l1_dil_w_out, l1_mlp_norm, l1_mlp_w_up, l1_mlp_w_down, l2_mix_norm, l2_mla_w_in, l2_mla_q_a_norm, l2_mla_kv_a_norm, l2_mla_w_q_b, l2_mla_w_kv_b, l2_mla_q_norm, l2_mla_k_norm, l2_mla_w_out, l2_mlp_norm, l2_mlp_w_up, l2_mlp_w_down, l3_mix_norm, l3_sb_w_in, l3_sb_q_norm, l3_sb_k_norm, l3_sb_w_out, l3_mlp_norm, l3_mlp_w_up, l3_mlp_w_down):
    layers = [
        (l0_mix_norm, (l0_sb_w_in, l0_sb_q_norm, l0_sb_k_norm, l0_sb_w_out),
         l0_mlp_norm, l0_mlp_w_up, l0_mlp_w_down),
        (l1_mix_norm, (l1_dil_w_in, l1_dil_q_norm, l1_dil_k_norm, l1_dil_w_out),
         l1_mlp_norm, l1_mlp_w_up, l1_mlp_w_down),
        (l2_mix_norm, (l2_mla_w_in, l2_mla_q_a_norm, l2_mla_kv_a_norm, l2_mla_w_q_b,
                       l2_mla_w_kv_b, l2_mla_q_norm, l2_mla_k_norm, l2_mla_w_out),
         l2_mlp_norm, l2_mlp_w_up, l2_mlp_w_down),
        (l3_mix_norm, (l3_sb_w_in, l3_sb_q_norm, l3_sb_k_norm, l3_sb_w_out),
         l3_mlp_norm, l3_mlp_w_up, l3_mlp_w_down),
    ]
    return _trunk(x, layers)
```

```python
import functools
import math

import jax
import jax.numpy as jnp
from jax import lax
from jax.experimental import pallas as pl
from jax.experimental.pallas import tpu as pltpu

F32 = jnp.float32
BF16 = jnp.bfloat16

EPS = 1e-6
LANES = 128
MXU_WIDTH = 256
MASKED = -1e30
LOG2_E = math.log2(math.e)
SB_SATURATED = 150.0
VMEM_LIMIT_BYTES = 48 * 1024 * 1024

D_MODEL = 1024
SB_HEADS = 16
SB_HEAD_DIM = 64
DIL_GROUPS = ((128, 1), (512, 4), (2048, 16))
DIL_HEADS = 8
DIL_HEAD_DIM = 128
DIL_BLOCK = 128
MLA_HEADS = 16
MLA_NOPE = 64
MLA_ROPE = 32
MLA_V = 64
MLA_QK = MLA_NOPE + MLA_ROPE
MLA_Q_RANK = 384
MLA_KV_RANK = 256
MLA_KEY_TILE = 256
MLA_HEADS_PER_STEP = 4
ROPE_THETA = 10000.0


def _params(*semantics):
    return pltpu.CompilerParams(dimension_semantics=semantics, vmem_limit_bytes=VMEM_LIMIT_BYTES)


def _rms(x, g):
    ms = jnp.mean(x * x, axis=-1, keepdims=True)
    return x * lax.rsqrt(ms + EPS) * g


def _split_dot(x, w):
    hi = x.astype(BF16)
    lo = (x - hi.astype(F32)).astype(BF16)
    return (jnp.dot(hi, w, preferred_element_type=F32)
            + jnp.dot(lo, w, preferred_element_type=F32))


def _dot_t(a, b):
    return lax.dot_general(a, b, (((1,), (1,)), ((), ())), preferred_element_type=F32)


def _proj_qkv_kernel(x_ref, g_ref, w_ref, gain_ref, seg_ref, o_ref, xn_ref, *y_scratch, head_dim, dilation):
    tm, width = xn_ref.shape[0], D_MODEL
    slabs = width // LANES
    seg_width = seg_ref.shape[0]
    xn_ref[...] = _rms(x_ref[...], g_ref[...]).astype(BF16)

    def project(block, lo):
        col = block * width + lo
        return jnp.dot(xn_ref[...], w_ref[:, col:col + seg_width], preferred_element_type=F32)

    def finish(block, lo, yc):
        if block < 2:
            ms = _split_dot(yc * yc, seg_ref[...]) * (1.0 / head_dim)
            yc = yc * lax.rsqrt(ms + EPS) * gain_ref[block:block + 1, lo:lo + seg_width]
        for c in range(seg_width // LANES):
            piece = yc[:, c * LANES:(c + 1) * LANES]
            if dilation == 1:
                o_ref[:, block * width + lo + c * LANES:block * width + lo + (c + 1) * LANES] = piece.astype(BF16)
            else:
                y_scratch[0][block * slabs + lo // LANES + c] = piece

    chunks = [(block, lo) for block in range(3) for lo in range(0, width, seg_width)]
    pending = project(*chunks[0])
    for previous, nxt in zip(chunks[:-1], chunks[1:]):
        following = project(*nxt)
        finish(*previous, pending)
        pending = following
    finish(*chunks[-1], pending)
    if dilation > 1:
        for block in range(3):
            for r in range(dilation):
                for c in range(slabs):
                    rows = y_scratch[0][block * slabs + c, pl.ds(r, tm // dilation, stride=dilation), :]
                    lo = (r * 3 + block) * width + c * LANES
                    o_ref[:, lo:lo + LANES] = rows.astype(BF16)


def _proj_qkv(h, norm_g, w, gains, head_dim, dilation=1, tm=512):
    t, d = h.shape
    width = D_MODEL
    assert w.shape == (d, 3 * width) and tm % (16 * dilation) == 0
    lane = jnp.arange(MXU_WIDTH)
    seg = (lane[:, None] // head_dim == lane[None, :] // head_dim).astype(BF16)
    return pl.pallas_call(
        functools.partial(_proj_qkv_kernel, head_dim=head_dim, dilation=dilation),
        grid=(t // tm,),
        in_specs=[
            pl.BlockSpec((tm, d), lambda i: (i, 0)),
            pl.BlockSpec((1, d), lambda i: (0, 0)),
            pl.BlockSpec((d, 3 * width), lambda i: (0, 0)),
            pl.BlockSpec((2, width), lambda i: (0, 0)),
            pl.BlockSpec((MXU_WIDTH, MXU_WIDTH), lambda i: (0, 0)),
        ],
        out_specs=pl.BlockSpec((tm // dilation, dilation * 3 * width), lambda i: (i, 0)),
        out_shape=jax.ShapeDtypeStruct((t // dilation, dilation * 3 * width), BF16),
        scratch_shapes=[pltpu.VMEM((tm, d), BF16)]
                       + ([pltpu.VMEM((3 * width // LANES, tm, LANES), F32)] if dilation > 1 else []),
        compiler_params=_params("parallel"),
        name="proj_qkv_d%d" % dilation,
    )(h, norm_g.reshape(1, d), w, gains, seg)


def _sb_attn_kernel(q_ref, k_ref, v_ref, tri_ref, o_ref, *, blk):
    i = pl.program_id(2)
    lane = lax.broadcasted_iota(jnp.int32, (1, LANES), 1)
    row = lax.broadcasted_iota(jnp.int32, (blk, blk), 0)
    col = lax.broadcasted_iota(jnp.int32, (blk, blk), 1)
    strictly_past = col < row
    q = q_ref[...]
    heads = LANES // SB_HEAD_DIM
    head_lanes = [(lane // SB_HEAD_DIM) == head for head in range(heads)]
    qs = [jnp.where(hl, q, jnp.zeros_like(q)) for hl in head_lanes]

    def tiles(js, masks, state):
        starts = [pl.multiple_of(jnp.maximum(j, 0) * blk, blk) for j in js]
        keys = [k_ref[pl.ds(start, blk), :] for start in starts]
        vals = [v_ref[pl.ds(start, blk), :] for start in starts]
        pairs = [(t, head) for t in range(len(js)) for head in range(heads)]
        zs = {}
        for t, head in pairs:
            z = _dot_t(qs[head], keys[t])
            zs[t, head] = z if masks[t] is None else jnp.where(masks[t], z, MASKED)
        sps = {p: jnp.maximum(zs[p], 0.0) + jnp.log2(1.0 + jnp.exp2(-jnp.abs(zs[p]))) for p in pairs}
        withins = {p: _split_dot(sps[p], tri_ref[...]) for p in pairs}
        later = [state[head][1] for head in range(heads)]
        probs = {}
        for t, head in pairs:
            probs[t, head] = jnp.exp2(zs[t, head] - withins[t, head] - later[head]).astype(BF16)
            later[head] = later[head] + withins[t, head][:, 0:1]
        acc = [state[head][0] for head in range(heads)]
        for t, head in pairs:
            acc[head] = acc[head] + jnp.dot(probs[t, head], vals[t], preferred_element_type=F32)
        return tuple((acc[head], later[head]) for head in range(heads))

    state = tuple((jnp.zeros((blk, LANES), F32), jnp.zeros((blk, 1), F32)) for _ in range(heads))
    state = tiles([i, i - 1], [strictly_past, i > 0], state)
    n_pairs = lax.shift_right_logical(jnp.maximum(i - 1, 0) + 1, 1)

    def unsaturated(s):
        lowest = jnp.min(jnp.minimum(s[0][1], s[1][1]))
        return (lowest < SB_SATURATED).astype(jnp.int32)

    def pair(carry):
        t, _, s = carry
        j = i - 2 - 2 * t
        s = tiles([j, j - 1], [None, j > 0], s)
        return t + 1, unsaturated(s), s

    _, _, state = lax.while_loop(lambda c: jnp.logical_and(c[0] < n_pairs, c[1] > 0), pair,
                                 (jnp.int32(0), unsaturated(state), state))
    out = sum(jnp.where(head_lanes[head], state[head][0], 0.0) for head in range(heads))
    o_ref[...] = out.astype(BF16)


def _sb_attention(qkv, batch, seq, blk=256):
    pairs = D_MODEL // LANES
    nq = seq // blk
    qkv3 = qkv.reshape(batch, seq, 3 * D_MODEL)
    idx = jnp.arange(blk)
    tri = (idx[:, None] >= idx[None, :]).astype(BF16)
    out = pl.pallas_call(
        functools.partial(_sb_attn_kernel, blk=blk),
        grid=(batch, pairs, nq),
        in_specs=[
            pl.BlockSpec((None, blk, LANES), lambda b, p, i: (b, i, p)),
            pl.BlockSpec((None, seq, LANES), lambda b, p, i: (b, 0, pairs + p)),
            pl.BlockSpec((None, seq, LANES), lambda b, p, i: (b, 0, 2 * pairs + p)),
            pl.BlockSpec((blk, blk), lambda b, p, i: (0, 0)),
        ],
        out_specs=pl.BlockSpec((None, blk, LANES), lambda b, p, i: (b, i, p)),
        out_shape=jax.ShapeDtypeStruct((batch, seq, D_MODEL), BF16),
        compiler_params=_params("parallel", "parallel", "arbitrary"),
        name="sb_attention",
    )(qkv3, qkv3, qkv3, tri)
    return out.reshape(batch * seq, D_MODEL)


def _out_proj_kernel(o_ref, w_ref, h_ref, out_ref):
    out_ref[...] = h_ref[...] + jnp.dot(o_ref[...], w_ref[...], preferred_element_type=F32)


def _out_proj(o, w, h, tm=1024):
    t, d = h.shape
    return pl.pallas_call(
        _out_proj_kernel,
        grid=(t // tm,),
        in_specs=[
            pl.BlockSpec((tm, o.shape[1]), lambda i: (i, 0)),
            pl.BlockSpec(w.shape, lambda i: (0, 0)),
            pl.BlockSpec((tm, d), lambda i: (i, 0)),
        ],
        out_specs=pl.BlockSpec((tm, d), lambda i: (i, 0)),
        out_shape=jax.ShapeDtypeStruct((t, d), F32),
        compiler_params=_params("parallel"),
        name="out_proj",
    )(o, w, h)


def _mlp_kernel(h_ref, g_ref, wu_ref, wd_ref, out_ref, xn_ref, acc_ref):
    f = pl.program_id(1)

    @pl.when(f == 0)
    def _():
        xn_ref[...] = _rms(h_ref[...], g_ref[...]).astype(BF16)
        acc_ref[...] = jnp.zeros_like(acc_ref)

    u = jnp.maximum(jnp.dot(xn_ref[...], wu_ref[...], preferred_element_type=F32), 0.0)
    acc_ref[...] += jnp.dot((u * u).astype(BF16), wd_ref[...], preferred_element_type=F32)

    @pl.when(f == pl.num_programs(1) - 1)
    def _():
        out_ref[...] = h_ref[...] + acc_ref[...]


def _mlp(h, norm_g, w_up, w_down, tm=1024, fc=512):
    t, d = h.shape
    ff = w_up.shape[1]
    return pl.pallas_call(
        _mlp_kernel,
        grid=(t // tm, ff // fc),
        in_specs=[
            pl.BlockSpec((tm, d), lambda i, f: (i, 0)),
            pl.BlockSpec((1, d), lambda i, f: (0, 0)),
            pl.BlockSpec((d, fc), lambda i, f: (0, f)),
            pl.BlockSpec((fc, d), lambda i, f: (f, 0)),
        ],
        out_specs=pl.BlockSpec((tm, d), lambda i, f: (i, 0)),
        out_shape=jax.ShapeDtypeStruct((t, d), F32),
        scratch_shapes=[pltpu.VMEM((tm, d), BF16), pltpu.VMEM((tm, d), F32)],
        compiler_params=_params("parallel", "arbitrary"),
        name="mlp",
    )(h, norm_g.reshape(1, d), w_up, w_down)


def _dil_attn_kernel(q_ref, kc_ref, kp_ref, vc_ref, vp_ref, o_ref, lse_ref, *, bias_per_step, max_steps):
    n = pl.program_id(2)
    blk = DIL_BLOCK
    qi = lax.broadcasted_iota(jnp.int32, (blk, 2 * blk), 0)
    kj = lax.broadcasted_iota(jnp.int32, (blk, 2 * blk), 1)
    steps = blk + qi - kj
    valid = (steps >= 0) & (steps <= max_steps) & ((kj >= blk) | (n > 0))
    steps_f = steps.astype(F32)
    lane = lax.broadcasted_iota(jnp.int32, (1, LANES), 1)
    lse_tile = jnp.zeros((blk, LANES), F32)
    for head in range(DIL_HEADS):
        sl = slice(head * DIL_HEAD_DIM, (head + 1) * DIL_HEAD_DIM)
        keys = jnp.concatenate([kp_ref[:, sl], kc_ref[:, sl]], axis=0)
        vals = jnp.concatenate([vp_ref[:, sl], vc_ref[:, sl]], axis=0)
        z = _dot_t(q_ref[:, sl], keys)
        z = jnp.where(valid, z - bias_per_step[head] * steps_f, MASKED)
        m = jnp.max(z, axis=-1, keepdims=True)
        p = jnp.exp(z - m)
        denom = jnp.sum(p, axis=-1, keepdims=True)
        o_ref[:, sl] = jnp.dot(p.astype(BF16), vals, preferred_element_type=F32) / denom
        lse_tile = jnp.where(lane == head, m + jnp.log(denom), lse_tile)
    lse_ref[...] = lse_tile


def _dil_attention(qkv, group, batch, seq):
    window, dilation = DIL_GROUPS[group]
    n_groups = len(DIL_GROUPS)
    blk = DIL_BLOCK
    length = seq // dilation
    assert length % blk == 0
    nb = length // blk
    width = DIL_HEADS * DIL_HEAD_DIM
    slopes = [2.0 ** (-8.0 * (group * DIL_HEADS + hd + 1) / (n_groups * DIL_HEADS)) for hd in range(DIL_HEADS)]
    view = qkv.reshape(batch, length, dilation * 3 * width)

    def col(offset):
        return lambda b, r, n: (b, n, r * 3 + offset)

    def col_prev(offset):
        return lambda b, r, n: (b, jnp.maximum(n - 1, 0), r * 3 + offset)

    o, lse = pl.pallas_call(
        functools.partial(_dil_attn_kernel,
                          bias_per_step=tuple(s * dilation for s in slopes),
                          max_steps=window // dilation),
        grid=(batch, dilation, nb),
        in_specs=[
            pl.BlockSpec((None, blk, width), col(0)),
            pl.BlockSpec((None, blk, width), col(1)),
            pl.BlockSpec((None, blk, width), col_prev(1)),
            pl.BlockSpec((None, blk, width), col(2)),
            pl.BlockSpec((None, blk, width), col_prev(2)),
        ],
        out_specs=[
            pl.BlockSpec((None, blk, width), lambda b, r, n: (b, n, r)),
            pl.BlockSpec((None, blk, LANES), lambda b, r, n: (b, n, r)),
        ],
        out_shape=[
            jax.ShapeDtypeStruct((batch, length, dilation * width), F32),
            jax.ShapeDtypeStruct((batch, length, dilation * LANES), F32),
        ],
        compiler_params=_params("parallel", "parallel", "arbitrary"),
        name="dil_attention_g%d" % group,
    )(view, view, view, view, view)
    rows = batch * length
    return o.reshape(rows, dilation * width), lse.reshape(rows, dilation * LANES)


def _dil_merge_kernel(o0_ref, o1_ref, o2_ref, l0_ref, l1_ref, l2_ref, w_ref, h_ref, out_ref,
                      o_tok_ref, l_tok_ref, mrg_ref):
    tm = h_ref.shape[0]
    slabs = D_MODEL // LANES
    o_in = (o0_ref, o1_ref, o2_ref)
    l_in = (l0_ref, l1_ref, l2_ref)
    for g, (_, dilation) in enumerate(DIL_GROUPS):
        for r in range(dilation):
            rows = slice(None) if dilation == 1 else pl.ds(r, tm // dilation, stride=dilation)
            l_tok_ref[g, rows, :] = l_in[g][:, r * LANES:(r + 1) * LANES]
            for c in range(slabs):
                lo = r * D_MODEL + c * LANES
                o_tok_ref[g * slabs + c, rows, :] = o_in[g][:, lo:lo + LANES]
    lses = [l_tok_ref[g] for g in range(len(DIL_GROUPS))]
    m = jnp.maximum(jnp.maximum(lses[0], lses[1]), lses[2])
    es = [jnp.exp(l - m) for l in lses]
    total = es[0] + es[1] + es[2]
    alphas = [e / total for e in es]
    for head in range(DIL_HEADS):
        merged = sum(alphas[g][:, head:head + 1] * o_tok_ref[g * slabs + head] for g in range(len(DIL_GROUPS)))
        mrg_ref[:, head * DIL_HEAD_DIM:(head + 1) * DIL_HEAD_DIM] = merged.astype(BF16)
    out_ref[...] = h_ref[...] + jnp.dot(mrg_ref[...], w_ref[...], preferred_element_type=F32)


def _dil_merge_out_proj(outs, lses, w, h, tm=512):
    t, d = h.shape
    n_groups = len(DIL_GROUPS)
    grouped = lambda cols, dil: pl.BlockSpec((tm // dil, dil * cols), lambda i: (i, 0))
    dils = [dil for _, dil in DIL_GROUPS]
    return pl.pallas_call(
        _dil_merge_kernel,
        grid=(t // tm,),
        in_specs=[grouped(d, dil) for dil in dils] + [grouped(LANES, dil) for dil in dils]
                 + [pl.BlockSpec(w.shape, lambda i: (0, 0)), pl.BlockSpec((tm, d), lambda i: (i, 0))],
        out_specs=pl.BlockSpec((tm, d), lambda i: (i, 0)),
        out_shape=jax.ShapeDtypeStruct((t, d), F32),
        scratch_shapes=[pltpu.VMEM((n_groups * d // LANES, tm, LANES), F32),
                        pltpu.VMEM((n_groups, tm, LANES), F32),
                        pltpu.VMEM((tm, d), BF16)],
        compiler_params=_params("parallel"),
        name="dil_merge_out_proj",
    )(*outs, *lses, w, h)


def _mla_proj_kernel(x_ref, g_ref, win_ref, qa_ref, kva_ref, wq_ref, wk_ref, wv_ref, qg_ref, kg_ref,
                     cos_ref, sin_up_ref, sin_dn_ref, ones_ref, q_out, k_out, v_out, qp_ref, kp_ref):
    xn = _rms(x_ref[...], g_ref[...]).astype(BF16)
    c = jnp.dot(xn, win_ref[...], preferred_element_type=F32)
    cq = _rms(c[:, :MLA_Q_RANK], qa_ref[...]).astype(BF16)
    ckv = _rms(c[:, MLA_Q_RANK:MLA_Q_RANK + MLA_KV_RANK], kva_ref[...]).astype(BF16)
    shared_rope = c[:, MLA_Q_RANK + MLA_KV_RANK:]
    qp_ref[...] = jnp.dot(cq, wq_ref[...], preferred_element_type=F32)
    kp_ref[...] = jnp.dot(ckv, wk_ref[...], preferred_element_type=F32)
    v = jnp.dot(ckv, wv_ref[...], preferred_element_type=F32)
    for s in range(v_out.shape[0]):
        v_out[s] = v[s * MLA_KEY_TILE:(s + 1) * MLA_KEY_TILE, :].T.astype(BF16)
    cos = cos_ref[...]
    sin_up = sin_up_ref[...]
    sin_dn = sin_dn_ref[...]
    half = MLA_ROPE // 2

    group = ones_ref.shape[0]

    def norm_rope(x, gain):
        ms = _split_dot(x * x, ones_ref[...]) * (1.0 / MLA_QK)
        y = x * lax.rsqrt(ms + EPS) * gain
        return y * cos + pltpu.roll(y, half, 1) * sin_up + pltpu.roll(y, group - half, 1) * sin_dn

    shared = jnp.concatenate([shared_rope] * (group // LANES), axis=1)
    for lo in range(0, MLA_HEADS * LANES, group):
        sl = slice(lo, lo + group)
        q_out[:, sl] = norm_rope(qp_ref[:, sl], qg_ref[...]).astype(BF16)
        k_out[:, sl] = norm_rope(kp_ref[:, sl] + shared, kg_ref[...]).astype(BF16)


def _pad_heads(w, heads, src_lo, src_hi, src_width):
    k = w.shape[0]
    w = w.reshape(k, heads, src_width)[:, :, src_lo:src_hi]
    w = jnp.pad(w, ((0, 0), (0, 0), (0, LANES - (src_hi - src_lo))))
    return w.reshape(k, heads * LANES)


def _mla_proj(h, norm_g, w_in, q_a_gain, kv_a_gain, w_q_b, w_kv_b, q_gain, k_gain, seq, tm=512):
    t, d = h.shape
    latent = MLA_Q_RANK + MLA_KV_RANK
    w_in_p = jnp.concatenate([
        w_in[:, :latent],
        jnp.zeros((d, MLA_NOPE), w_in.dtype),
        w_in[:, latent:],
        jnp.zeros((d, LANES - MLA_QK), w_in.dtype)], axis=1).astype(BF16)
    wq = _pad_heads(w_q_b, MLA_HEADS, 0, MLA_QK, MLA_QK).astype(BF16)
    wk = _pad_heads(w_kv_b, MLA_HEADS, 0, MLA_NOPE, MLA_NOPE + MLA_V).astype(BF16)
    wv = w_kv_b.reshape(MLA_KV_RANK, MLA_HEADS, MLA_NOPE + MLA_V)[:, :, MLA_NOPE:]
    wv = wv.reshape(MLA_KV_RANK, MLA_HEADS * MLA_V).astype(BF16)
    pad = jnp.zeros((LANES - MLA_QK,), F32)
    group = MXU_WIDTH // LANES
    qg = jnp.tile(jnp.concatenate([q_gain, pad]) * (LOG2_E / math.sqrt(MLA_QK)), group).reshape(1, MXU_WIDTH)
    kg = jnp.tile(jnp.concatenate([k_gain, pad]), group).reshape(1, MXU_WIDTH)
    half = MLA_ROPE // 2
    inv = ROPE_THETA ** (-jnp.arange(half, dtype=F32) / half)
    ang = jnp.arange(seq, dtype=F32)[:, None] * inv[None, :]
    zeros = jnp.zeros((seq, half), F32)
    cos = jnp.concatenate([jnp.ones((seq, MLA_NOPE), F32), jnp.cos(ang), jnp.cos(ang),
                           jnp.zeros((seq, LANES - MLA_QK), F32)], axis=1)
    sin_up = jnp.concatenate([jnp.zeros((seq, MLA_NOPE), F32), zeros, jnp.sin(ang),
                              jnp.zeros((seq, LANES - MLA_QK), F32)], axis=1)
    sin_dn = jnp.concatenate([jnp.zeros((seq, MLA_NOPE), F32), -jnp.sin(ang), zeros,
                              jnp.zeros((seq, LANES - MLA_QK), F32)], axis=1)
    cos, sin_up, sin_dn = (jnp.tile(table, (1, group)) for table in (cos, sin_up, sin_dn))
    lane = jnp.arange(MXU_WIDTH)
    ones = (lane[:, None] // LANES == lane[None, :] // LANES).astype(BF16)
    per_seq = seq // tm
    full = lambda a: pl.BlockSpec(a.shape, lambda i: (0,) * a.ndim)
    table = pl.BlockSpec((tm, MXU_WIDTH), lambda i: (i % per_seq, 0))
    consts = (norm_g.reshape(1, d), w_in_p, q_a_gain.reshape(1, -1), kv_a_gain.reshape(1, -1), wq, wk, wv, qg, kg)
    qk_cols = MLA_HEADS * LANES
    v_rows = MLA_HEADS * MLA_V
    slabs = tm // MLA_KEY_TILE
    return pl.pallas_call(
        _mla_proj_kernel,
        grid=(t // tm,),
        in_specs=[pl.BlockSpec((tm, d), lambda i: (i, 0))] + [full(a) for a in consts]
                 + [table, table, table, full(ones)],
        out_specs=[
            pl.BlockSpec((tm, qk_cols), lambda i: (i, 0)),
            pl.BlockSpec((tm, qk_cols), lambda i: (i, 0)),
            pl.BlockSpec((slabs, v_rows, MLA_KEY_TILE), lambda i: (i, 0, 0)),
        ],
        out_shape=[
            jax.ShapeDtypeStruct((t, qk_cols), BF16),
            jax.ShapeDtypeStruct((t, qk_cols), BF16),
            jax.ShapeDtypeStruct((t // MLA_KEY_TILE, v_rows, MLA_KEY_TILE), BF16),
        ],
        scratch_shapes=[pltpu.VMEM((tm, qk_cols), F32), pltpu.VMEM((tm, qk_cols), F32)],
        compiler_params=_params("parallel"),
        name="mla_proj",
    )(h, *consts, cos, sin_up, sin_dn, ones)


def _mla_attn_kernel(q_ref, k_ref, vt_ref, o_ref):
    i = pl.program_id(2)
    heads = MLA_HEADS_PER_STEP
    sub = MLA_KEY_TILE
    bq = q_ref.shape[0]
    n_sub = bq // sub
    lanes = [slice(head * LANES, (head + 1) * LANES) for head in range(heads)]
    qs = [q_ref[:, sl] for sl in lanes]
    key_idx = lax.broadcasted_iota(jnp.int32, (bq, bq), 0)
    qry_idx = lax.broadcasted_iota(jnp.int32, (bq, bq), 1)

    def tile(j, state, diagonal):
        start = pl.multiple_of(j * bq, bq)
        keys = k_ref[pl.ds(start, bq), :]
        sts = [_dot_t(keys[:, lanes[head]], qs[head]) for head in range(heads)]
        if diagonal:
            sts = [jnp.where(key_idx <= qry_idx, st, MASKED) for st in sts]
        ms = [jnp.maximum(state[head][0], jnp.max(sts[head], axis=0, keepdims=True)) for head in range(heads)]
        pts = [jnp.exp2(sts[head] - ms[head]) for head in range(heads)]
        new_state = []
        for head in range(heads):
            m, l, acc = state[head]
            corr = jnp.exp2(m - ms[head])
            pv = sum(jnp.dot(vt_ref[j * n_sub + s, head * MLA_V:(head + 1) * MLA_V, :],
                             pts[head][s * sub:(s + 1) * sub].astype(BF16), preferred_element_type=F32)
                     for s in range(n_sub))
            new_state.append((ms[head], corr * l + jnp.sum(pts[head], axis=0, keepdims=True), corr * acc + pv))
        return tuple(new_state)

    state = tuple((jnp.full((1, bq), MASKED, F32), jnp.zeros((1, bq), F32), jnp.zeros((MLA_V, bq), F32))
                  for _ in range(heads))
    state = tile(i, state, True)
    state = lax.fori_loop(0, i, lambda t, s: tile(t, s, False), state)
    out_t = jnp.concatenate([state[head][2] / state[head][1] for head in range(heads)], axis=0)
    o_ref[...] = out_t.T.astype(BF16)


def _mla_attention(q, k, vt, batch, seq, bq=512):
    heads = MLA_HEADS_PER_STEP
    sub = MLA_KEY_TILE
    q3 = q.reshape(batch, seq, -1)
    k3 = k.reshape(batch, seq, -1)
    vt4 = vt.reshape(batch, seq // sub, MLA_HEADS * MLA_V, sub)
    out = pl.pallas_call(
        _mla_attn_kernel,
        grid=(batch, MLA_HEADS // heads, seq // bq),
        in_specs=[
            pl.BlockSpec((None, bq, heads * LANES), lambda b, p, i: (b, i, p)),
            pl.BlockSpec((None, seq, heads * LANES), lambda b, p, i: (b, 0, p)),
            pl.BlockSpec((None, seq // sub, heads * MLA_V, sub), lambda b, p, i: (b, 0, p, 0)),
        ],
        out_specs=pl.BlockSpec((None, bq, heads * MLA_V), lambda b, p, i: (b, i, p)),
        out_shape=jax.ShapeDtypeStruct((batch, seq, MLA_HEADS * MLA_V), BF16),
        compiler_params=_params("parallel", "parallel", "arbitrary"),
        name="mla_attention",
    )(q3, k3, vt4)
    return out.reshape(batch * seq, MLA_HEADS * MLA_V)


def _sb_layer(h, batch, seq, mix_norm, w_in, q_norm, k_norm, w_out):
    scale = LOG2_E / math.sqrt(SB_HEAD_DIM)
    gains = jnp.stack([jnp.tile(q_norm, SB_HEADS) * scale, jnp.tile(k_norm, SB_HEADS)])
    qkv = _proj_qkv(h, mix_norm, w_in.astype(BF16), gains, SB_HEAD_DIM)
    o = _sb_attention(qkv, batch, seq)
    return _out_proj(o, w_out.astype(BF16), h)


def _dil_layer(h, batch, seq, mix_norm, w_in, q_norm, k_norm, w_out):
    scale = DIL_HEAD_DIM ** -0.5
    w_in = w_in.astype(BF16)
    cols = 3 * D_MODEL
    outs, lses = [], []
    for g, (_, dilation) in enumerate(DIL_GROUPS):
        gains = jnp.stack([jnp.tile(q_norm[g], DIL_HEADS) * scale, jnp.tile(k_norm[g], DIL_HEADS)])
        qkv = _proj_qkv(h, mix_norm, w_in[:, g * cols:(g + 1) * cols], gains, DIL_HEAD_DIM, dilation)
        o, lse = _dil_attention(qkv, g, batch, seq)
        outs.append(o)
        lses.append(lse)
    return _dil_merge_out_proj(outs, lses, w_out.astype(BF16), h)


def _mla_layer(h, batch, seq, mix_norm, w_in, q_a_norm, kv_a_norm, w_q_b, w_kv_b, q_norm, k_norm, w_out):
    q, k, v = _mla_proj(h, mix_norm, w_in, q_a_norm, kv_a_norm, w_q_b, w_kv_b, q_norm, k_norm, seq)
    o = _mla_attention(q, k, v, batch, seq)
    return _out_proj(o, w_out.astype(BF16), h)


def _trunk(x, layers):
    batch, seq, d = x.shape
    h = x.reshape(batch * seq, d)
    mixers = (_sb_layer, _dil_layer, _mla_layer)
    for idx, (mix_norm, mix_params, mlp_norm, w_up, w_down) in enumerate(layers):
        h = mixers[idx % len(mixers)](h, batch, seq, mix_norm, *mix_params)
        h = _mlp(h, mlp_norm, w_up.astype(BF16), w_down.astype(BF16))
    return h.reshape(batch, seq, d)


def kernel(x, l0_mix_norm, l0_sb_w_in, l0_sb_q_norm, l0_sb_k_norm, l0_sb_w_out, l0_mlp_norm, l0_mlp_w_up, l0_mlp_w_down, l1_mix_norm, l1_dil_w_in, l1_dil_q_norm, l1_dil_k_norm, l1_dil_w_out, l1_mlp_norm, l1_mlp_w_up, l1_mlp_w_down, l2_mix_norm, l2_mla_w_in, l2_mla_q_a_norm, l2_mla_kv_a_norm, l2_mla_w_q_b, l2_mla_w_kv_b, l2_mla_q_norm, l2_mla_k_norm, l2_mla_w_out, l2_mlp_norm, l2_mlp_w_up, l2_mlp_w_down, l3_mix_norm, l3_sb_w_in, l3_sb_q_norm, l3_sb_k_norm, l3_sb_w_out, l3_mlp_norm, l3_mlp_w_up, l3_mlp_w_down):
    layers = [
        (l0_mix_norm, (l0_sb_w_in, l0_sb_q_norm, l0_sb_k_norm, l0_sb_w_out),
         l0_mlp_norm, l0_mlp_w_up, l0_mlp_w_down),
        (l1_mix_norm, (l1_dil_w_in, l1_dil_q_norm, l1_dil_k_norm, l1_dil_w_out),
         l1_mlp_norm, l1_mlp_w_up, l1_mlp_w_down),
        (l2_mix_norm, (l2_mla_w_in, l2_mla_q_a_norm, l2_mla_kv_a_norm, l2_mla_w_q_b,
                       l2_mla_w_kv_b, l2_mla_q_norm, l2_mla_k_norm, l2_mla_w_out),
         l2_mlp_norm, l2_mlp_w_up, l2_mlp_w_down),
        (l3_mix_norm, (l3_sb_w_in, l3_sb_q_norm, l3_sb_k_norm, l3_sb_w_out),
         l3_mlp_norm, l3_mlp_w_up, l3_mlp_w_down),
    ]
    return _trunk(x, layers)
```

```python
import functools
import math

import jax
import jax.numpy as jnp
from jax import lax
from jax.experimental import pallas as pl
from jax.experimental.pallas import tpu as pltpu

F32 = jnp.float32
BF16 = jnp.bfloat16

EPS = 1e-6
LANES = 128
MXU_WIDTH = 256
MASKED = -1e30
LOG2_E = math.log2(math.e)
SB_SATURATED = 150.0
VMEM_LIMIT_BYTES = 48 * 1024 * 1024

D_MODEL = 1024
SB_HEADS = 16
SB_HEAD_DIM = 64
DIL_GROUPS = ((128, 1), (512, 4), (2048, 16))
DIL_HEADS = 8
DIL_HEAD_DIM = 128
DIL_BLOCK = 128
MLA_HEADS = 16
MLA_NOPE = 64
MLA_ROPE = 32
MLA_V = 64
MLA_QK = MLA_NOPE + MLA_ROPE
MLA_Q_RANK = 384
MLA_KV_RANK = 256
MLA_KEY_TILE = 256
MLA_HEADS_PER_STEP = 4
ROPE_THETA = 10000.0


def _params(*semantics):
    return pltpu.CompilerParams(dimension_semantics=semantics, vmem_limit_bytes=VMEM_LIMIT_BYTES)


def _rms(x, g):
    ms = jnp.mean(x * x, axis=-1, keepdims=True)
    return x * lax.rsqrt(ms + EPS) * g


def _split_dot(x, w):
    hi = x.astype(BF16)
    lo = (x - hi.astype(F32)).astype(BF16)
    return (jnp.dot(hi, w, preferred_element_type=F32)
            + jnp.dot(lo, w, preferred_element_type=F32))


def _dot_t(a, b):
    return lax.dot_general(a, b, (((1,), (1,)), ((), ())), preferred_element_type=F32)


def _proj_qkv_kernel(x_ref, g_ref, w_ref, gain_ref, seg_ref, o_ref, xn_ref, *y_scratch, head_dim, dilation):
    tm, width = xn_ref.shape[0], D_MODEL
    slabs = width // LANES
    seg_width = seg_ref.shape[0]
    xn_ref[...] = _rms(x_ref[...], g_ref[...]).astype(BF16)

    def project(block, lo):
        col = block * width + lo
        return jnp.dot(xn_ref[...], w_ref[:, col:col + seg_width], preferred_element_type=F32)

    def finish(block, lo, yc):
        if block < 2:
            ms = _split_dot(yc * yc, seg_ref[...]) * (1.0 / head_dim)
            yc = yc * lax.rsqrt(ms + EPS) * gain_ref[block:block + 1, lo:lo + seg_width]
        for c in range(seg_width // LANES):
            piece = yc[:, c * LANES:(c + 1) * LANES]
            if dilation == 1:
                o_ref[:, block * width + lo + c * LANES:block * width + lo + (c + 1) * LANES] = piece.astype(BF16)
            else:
                y_scratch[0][block * slabs + lo // LANES + c] = piece

    chunks = [(block, lo) for block in range(3) for lo in range(0, width, seg_width)]
    pending = project(*chunks[0])
    for previous, nxt in zip(chunks[:-1], chunks[1:]):
        following = project(*nxt)
        finish(*previous, pending)
        pending = following
    finish(*chunks[-1], pending)
    if dilation > 1:
        for block in range(3):
            for r in range(dilation):
                for c in range(slabs):
                    rows = y_scratch[0][block * slabs + c, pl.ds(r, tm // dilation, stride=dilation), :]
                    lo = (r * 3 + block) * width + c * LANES
                    o_ref[:, lo:lo + LANES] = rows.astype(BF16)


def _proj_qkv(h, norm_g, w, gains, head_dim, dilation=1, tm=512):
    t, d = h.shape
    width = D_MODEL
    assert w.shape == (d, 3 * width) and tm % (16 * dilation) == 0
    lane = jnp.arange(MXU_WIDTH)
    seg = (lane[:, None] // head_dim == lane[None, :] // head_dim).astype(BF16)
    return pl.pallas_call(
        functools.partial(_proj_qkv_kernel, head_dim=head_dim, dilation=dilation),
        grid=(t // tm,),
        in_specs=[
            pl.BlockSpec((tm, d), lambda i: (i, 0)),
            pl.BlockSpec((1, d), lambda i: (0, 0)),
            pl.BlockSpec((d, 3 * width), lambda i: (0, 0)),
            pl.BlockSpec((2, width), lambda i: (0, 0)),
            pl.BlockSpec((MXU_WIDTH, MXU_WIDTH), lambda i: (0, 0)),
        ],
        out_specs=pl.BlockSpec((tm // dilation, dilation * 3 * width), lambda i: (i, 0)),
        out_shape=jax.ShapeDtypeStruct((t // dilation, dilation * 3 * width), BF16),
        scratch_shapes=[pltpu.VMEM((tm, d), BF16)]
                       + ([pltpu.VMEM((3 * width // LANES, tm, LANES), F32)] if dilation > 1 else []),
        compiler_params=_params("parallel"),
        name="proj_qkv_d%d" % dilation,
    )(h, norm_g.reshape(1, d), w, gains, seg)


def _sb_attn_kernel(q_ref, k_ref, v_ref, tri_ref, o_ref, *, blk):
    i = pl.program_id(2)
    lane = lax.broadcasted_iota(jnp.int32, (1, LANES), 1)
    row = lax.broadcasted_iota(jnp.int32, (blk, blk), 0)
    col = lax.broadcasted_iota(jnp.int32, (blk, blk), 1)
    strictly_past = col < row
    per_group = LANES // SB_HEAD_DIM
    groups = q_ref.shape[1] // LANES
    heads = groups * per_group
    group_lanes = [slice((head // per_group) * LANES, (head // per_group + 1) * LANES) for head in range(heads)]
    head_lanes = [(lane // SB_HEAD_DIM) == head % per_group for head in range(heads)]
    qs = [jnp.where(head_lanes[head], q_ref[:, group_lanes[head]], jnp.zeros((blk, LANES), BF16))
          for head in range(heads)]

    def tiles(js, masks, state):
        starts = [pl.multiple_of(jnp.maximum(j, 0) * blk, blk) for j in js]
        pairs = [(t, head) for t in range(len(js)) for head in range(heads)]
        zs = {}
        for t, head in pairs:
            z = _dot_t(qs[head], k_ref[pl.ds(starts[t], blk), group_lanes[head]])
            zs[t, head] = z if masks[t] is None else jnp.where(masks[t], z, MASKED)
        sps = {p: jnp.maximum(zs[p], 0.0) + jnp.log2(1.0 + jnp.exp2(-jnp.abs(zs[p]))) for p in pairs}
        withins = {p: _split_dot(sps[p], tri_ref[...]) for p in pairs}
        later = [state[head][1] for head in range(heads)]
        probs = {}
        for t, head in pairs:
            probs[t, head] = jnp.exp2(zs[t, head] - withins[t, head] - later[head]).astype(BF16)
            later[head] = later[head] + withins[t, head][:, 0:1]
        acc = [state[head][0] for head in range(heads)]
        for t, head in pairs:
            vals = v_ref[pl.ds(starts[t], blk), group_lanes[head]]
            acc[head] = acc[head] + jnp.dot(probs[t, head], vals, preferred_element_type=F32)
        return tuple((acc[head], later[head]) for head in range(heads))

    state = tuple((jnp.zeros((blk, LANES), F32), jnp.zeros((blk, 1), F32)) for _ in range(heads))
    state = tiles([i, i - 1], [strictly_past, i > 0], state)
    remaining = jnp.maximum(i - 1, 0)

    def unsaturated(s):
        lowest = functools.reduce(jnp.minimum, [later for _, later in s])
        return (jnp.min(lowest) < SB_SATURATED).astype(jnp.int32)

    def one_more(carry):
        t, _, s = carry
        s = tiles([i - 2 - t], [None], s)
        return t + 1, unsaturated(s), s

    _, _, state = lax.while_loop(lambda c: jnp.logical_and(c[0] < remaining, c[1] > 0), one_more,
                                 (jnp.int32(0), unsaturated(state), state))
    for group in range(groups):
        members = range(group * per_group, (group + 1) * per_group)
        out = sum(jnp.where(head_lanes[head], state[head][0], 0.0) for head in members)
        o_ref[:, group * LANES:(group + 1) * LANES] = out.astype(BF16)


def _sb_attention(qkv, batch, seq, blk=256, groups=2):
    width = groups * LANES
    steps = D_MODEL // width
    nq = seq // blk
    qkv3 = qkv.reshape(batch, seq, 3 * D_MODEL)
    idx = jnp.arange(blk)
    tri = (idx[:, None] >= idx[None, :]).astype(BF16)
    out = pl.pallas_call(
        functools.partial(_sb_attn_kernel, blk=blk),
        grid=(batch, steps, nq),
        in_specs=[
            pl.BlockSpec((None, blk, width), lambda b, p, i: (b, i, p)),
            pl.BlockSpec((None, seq, width), lambda b, p, i: (b, 0, steps + p)),
            pl.BlockSpec((None, seq, width), lambda b, p, i: (b, 0, 2 * steps + p)),
            pl.BlockSpec((blk, blk), lambda b, p, i: (0, 0)),
        ],
        out_specs=pl.BlockSpec((None, blk, width), lambda b, p, i: (b, i, p)),
        out_shape=jax.ShapeDtypeStruct((batch, seq, D_MODEL), BF16),
        compiler_params=_params("parallel", "parallel", "arbitrary"),
        name="sb_attention",
    )(qkv3, qkv3, qkv3, tri)
    return out.reshape(batch * seq, D_MODEL)


def _out_proj_kernel(o_ref, w_ref, h_ref, out_ref):
    out_ref[...] = h_ref[...] + jnp.dot(o_ref[...], w_ref[...], preferred_element_type=F32)


def _out_proj(o, w, h, tm=1024):
    t, d = h.shape
    return pl.pallas_call(
        _out_proj_kernel,
        grid=(t // tm,),
        in_specs=[
            pl.BlockSpec((tm, o.shape[1]), lambda i: (i, 0)),
            pl.BlockSpec(w.shape, lambda i: (0, 0)),
            pl.BlockSpec((tm, d), lambda i: (i, 0)),
        ],
        out_specs=pl.BlockSpec((tm, d), lambda i: (i, 0)),
        out_shape=jax.ShapeDtypeStruct((t, d), F32),
        compiler_params=_params("parallel"),
        name="out_proj",
    )(o, w, h)


def _mlp_kernel(h_ref, g_ref, wu_ref, wd_ref, out_ref, xn_ref, acc_ref):
    f = pl.program_id(1)

    @pl.when(f == 0)
    def _():
        xn_ref[...] = _rms(h_ref[...], g_ref[...]).astype(BF16)
        acc_ref[...] = jnp.zeros_like(acc_ref)

    u = jnp.maximum(jnp.dot(xn_ref[...], wu_ref[...], preferred_element_type=F32), 0.0)
    acc_ref[...] += jnp.dot((u * u).astype(BF16), wd_ref[...], preferred_element_type=F32)

    @pl.when(f == pl.num_programs(1) - 1)
    def _():
        out_ref[...] = h_ref[...] + acc_ref[...]


def _mlp(h, norm_g, w_up, w_down, tm=1024, fc=1024):
    t, d = h.shape
    ff = w_up.shape[1]
    return pl.pallas_call(
        _mlp_kernel,
        grid=(t // tm, ff // fc),
        in_specs=[
            pl.BlockSpec((tm, d), lambda i, f: (i, 0)),
            pl.BlockSpec((1, d), lambda i, f: (0, 0)),
            pl.BlockSpec((d, fc), lambda i, f: (0, f)),
            pl.BlockSpec((fc, d), lambda i, f: (f, 0)),
        ],
        out_specs=pl.BlockSpec((tm, d), lambda i, f: (i, 0)),
        out_shape=jax.ShapeDtypeStruct((t, d), F32),
        scratch_shapes=[pltpu.VMEM((tm, d), BF16), pltpu.VMEM((tm, d), F32)],
        compiler_params=_params("parallel", "arbitrary"),
        name="mlp",
    )(h, norm_g.reshape(1, d), w_up, w_down)


def _dil_attn_kernel(q_ref, kc_ref, kp_ref, vc_ref, vp_ref, o_ref, lse_ref, *, bias_per_step, max_steps):
    n = pl.program_id(2)
    blk = DIL_BLOCK
    qi = lax.broadcasted_iota(jnp.int32, (blk, 2 * blk), 0)
    kj = lax.broadcasted_iota(jnp.int32, (blk, 2 * blk), 1)
    steps = blk + qi - kj
    valid = (steps >= 0) & (steps <= max_steps) & ((kj >= blk) | (n > 0))
    steps_f = steps.astype(F32)
    lane = lax.broadcasted_iota(jnp.int32, (1, LANES), 1)
    lse_tile = jnp.zeros((blk, LANES), F32)
    lanes = [slice(head * DIL_HEAD_DIM, (head + 1) * DIL_HEAD_DIM) for head in range(DIL_HEADS)]
    zs = [_dot_t(q_ref[:, sl], jnp.concatenate([kp_ref[:, sl], kc_ref[:, sl]], axis=0)) for sl in lanes]
    zs = [jnp.where(valid, z - bias_per_step[head] * steps_f, MASKED) for head, z in enumerate(zs)]
    ms = [jnp.max(z, axis=-1, keepdims=True) for z in zs]
    ps = [jnp.exp(z - m) for z, m in zip(zs, ms)]
    denoms = [jnp.sum(p, axis=-1, keepdims=True) for p in ps]
    pvs = [jnp.dot(p.astype(BF16), jnp.concatenate([vp_ref[:, sl], vc_ref[:, sl]], axis=0),
                   preferred_element_type=F32) for p, sl in zip(ps, lanes)]
    for head, sl in enumerate(lanes):
        o_ref[:, sl] = pvs[head] / denoms[head]
        lse_tile = jnp.where(lane == head, ms[head] + jnp.log(denoms[head]), lse_tile)
    lse_ref[...] = lse_tile


def _dil_attention(qkv, group, batch, seq):
    window, dilation = DIL_GROUPS[group]
    n_groups = len(DIL_GROUPS)
    blk = DIL_BLOCK
    length = seq // dilation
    assert length % blk == 0
    nb = length // blk
    width = DIL_HEADS * DIL_HEAD_DIM
    slopes = [2.0 ** (-8.0 * (group * DIL_HEADS + hd + 1) / (n_groups * DIL_HEADS)) for hd in range(DIL_HEADS)]
    view = qkv.reshape(batch, length, dilation * 3 * width)

    def col(offset):
        return lambda b, r, n: (b, n, r * 3 + offset)

    def col_prev(offset):
        return lambda b, r, n: (b, jnp.maximum(n - 1, 0), r * 3 + offset)

    o, lse = pl.pallas_call(
        functools.partial(_dil_attn_kernel,
                          bias_per_step=tuple(s * dilation for s in slopes),
                          max_steps=window // dilation),
        grid=(batch, dilation, nb),
        in_specs=[
            pl.BlockSpec((None, blk, width), col(0)),
            pl.BlockSpec((None, blk, width), col(1)),
            pl.BlockSpec((None, blk, width), col_prev(1)),
            pl.BlockSpec((None, blk, width), col(2)),
            pl.BlockSpec((None, blk, width), col_prev(2)),
        ],
        out_specs=[
            pl.BlockSpec((None, blk, width), lambda b, r, n: (b, n, r)),
            pl.BlockSpec((None, blk, LANES), lambda b, r, n: (b, n, r)),
        ],
        out_shape=[
            jax.ShapeDtypeStruct((batch, length, dilation * width), F32),
            jax.ShapeDtypeStruct((batch, length, dilation * LANES), F32),
        ],
        compiler_params=_params("parallel", "parallel", "arbitrary"),
        name="dil_attention_g%d" % group,
    )(view, view, view, view, view)
    rows = batch * length
    return o.reshape(rows, dilation * width), lse.reshape(rows, dilation * LANES)


def _dil_merge_kernel(o0_ref, o1_ref, o2_ref, l0_ref, l1_ref, l2_ref, w_ref, h_ref, out_ref,
                      o_tok_ref, l_tok_ref, mrg_ref):
    tm = h_ref.shape[0]
    slabs = D_MODEL // LANES
    o_in = (o0_ref, o1_ref, o2_ref)
    l_in = (l0_ref, l1_ref, l2_ref)
    for g, (_, dilation) in enumerate(DIL_GROUPS):
        for r in range(dilation):
            rows = slice(None) if dilation == 1 else pl.ds(r, tm // dilation, stride=dilation)
            l_tok_ref[g, rows, :] = l_in[g][:, r * LANES:(r + 1) * LANES]
            for c in range(slabs):
                lo = r * D_MODEL + c * LANES
                o_tok_ref[g * slabs + c, rows, :] = o_in[g][:, lo:lo + LANES]
    lses = [l_tok_ref[g] for g in range(len(DIL_GROUPS))]
    m = jnp.maximum(jnp.maximum(lses[0], lses[1]), lses[2])
    es = [jnp.exp(l - m) for l in lses]
    total = es[0] + es[1] + es[2]
    alphas = [e / total for e in es]
    for head in range(DIL_HEADS):
        merged = sum(alphas[g][:, head:head + 1] * o_tok_ref[g * slabs + head] for g in range(len(DIL_GROUPS)))
        mrg_ref[:, head * DIL_HEAD_DIM:(head + 1) * DIL_HEAD_DIM] = merged.astype(BF16)
    out_ref[...] = h_ref[...] + jnp.dot(mrg_ref[...], w_ref[...], preferred_element_type=F32)


def _dil_merge_out_proj(outs, lses, w, h, tm=512):
    t, d = h.shape
    n_groups = len(DIL_GROUPS)
    grouped = lambda cols, dil: pl.BlockSpec((tm // dil, dil * cols), lambda i: (i, 0))
    dils = [dil for _, dil in DIL_GROUPS]
    return pl.pallas_call(
        _dil_merge_kernel,
        grid=(t // tm,),
        in_specs=[grouped(d, dil) for dil in dils] + [grouped(LANES, dil) for dil in dils]
                 + [pl.BlockSpec(w.shape, lambda i: (0, 0)), pl.BlockSpec((tm, d), lambda i: (i, 0))],
        out_specs=pl.BlockSpec((tm, d), lambda i: (i, 0)),
        out_shape=jax.ShapeDtypeStruct((t, d), F32),
        scratch_shapes=[pltpu.VMEM((n_groups * d // LANES, tm, LANES), F32),
                        pltpu.VMEM((n_groups, tm, LANES), F32),
                        pltpu.VMEM((tm, d), BF16)],
        compiler_params=_params("parallel"),
        name="dil_merge_out_proj",
    )(*outs, *lses, w, h)


def _mla_proj_kernel(x_ref, g_ref, win_ref, qa_ref, kva_ref, wq_ref, wk_ref, wv_ref, qg_ref, kg_ref,
                     cos_ref, sin_up_ref, sin_dn_ref, ones_ref, q_out, k_out, v_out, qp_ref, kp_ref):
    xn = _rms(x_ref[...], g_ref[...]).astype(BF16)
    c = jnp.dot(xn, win_ref[...], preferred_element_type=F32)
    cq = _rms(c[:, :MLA_Q_RANK], qa_ref[...]).astype(BF16)
    ckv = _rms(c[:, MLA_Q_RANK:MLA_Q_RANK + MLA_KV_RANK], kva_ref[...]).astype(BF16)
    shared_rope = c[:, MLA_Q_RANK + MLA_KV_RANK:]
    qp_ref[...] = jnp.dot(cq, wq_ref[...], preferred_element_type=F32)
    kp_ref[...] = jnp.dot(ckv, wk_ref[...], preferred_element_type=F32)
    v = jnp.dot(ckv, wv_ref[...], preferred_element_type=F32)
    for s in range(v_out.shape[0]):
        v_out[s] = v[s * MLA_KEY_TILE:(s + 1) * MLA_KEY_TILE, :].T.astype(BF16)
    cos = cos_ref[...]
    sin_up = sin_up_ref[...]
    sin_dn = sin_dn_ref[...]
    half = MLA_ROPE // 2

    group = ones_ref.shape[0]

    def norm_rope(x, gain):
        ms = _split_dot(x * x, ones_ref[...]) * (1.0 / MLA_QK)
        y = x * lax.rsqrt(ms + EPS) * gain
        return y * cos + pltpu.roll(y, half, 1) * sin_up + pltpu.roll(y, group - half, 1) * sin_dn

    shared = jnp.concatenate([shared_rope] * (group // LANES), axis=1)
    for lo in range(0, MLA_HEADS * LANES, group):
        sl = slice(lo, lo + group)
        q_out[:, sl] = norm_rope(qp_ref[:, sl], qg_ref[...]).astype(BF16)
        k_out[:, sl] = norm_rope(kp_ref[:, sl] + shared, kg_ref[...]).astype(BF16)


def _pad_heads(w, heads, src_lo, src_hi, src_width):
    k = w.shape[0]
    w = w.reshape(k, heads, src_width)[:, :, src_lo:src_hi]
    w = jnp.pad(w, ((0, 0), (0, 0), (0, LANES - (src_hi - src_lo))))
    return w.reshape(k, heads * LANES)


def _mla_proj(h, norm_g, w_in, q_a_gain, kv_a_gain, w_q_b, w_kv_b, q_gain, k_gain, seq, tm=512):
    t, d = h.shape
    latent = MLA_Q_RANK + MLA_KV_RANK
    w_in_p = jnp.concatenate([
        w_in[:, :latent],
        jnp.zeros((d, MLA_NOPE), w_in.dtype),
        w_in[:, latent:],
        jnp.zeros((d, LANES - MLA_QK), w_in.dtype)], axis=1).astype(BF16)
    wq = _pad_heads(w_q_b, MLA_HEADS, 0, MLA_QK, MLA_QK).astype(BF16)
    wk = _pad_heads(w_kv_b, MLA_HEADS, 0, MLA_NOPE, MLA_NOPE + MLA_V).astype(BF16)
    wv = w_kv_b.reshape(MLA_KV_RANK, MLA_HEADS, MLA_NOPE + MLA_V)[:, :, MLA_NOPE:]
    wv = wv.reshape(MLA_KV_RANK, MLA_HEADS * MLA_V).astype(BF16)
    pad = jnp.zeros((LANES - MLA_QK,), F32)
    group = MXU_WIDTH // LANES
    qg = jnp.tile(jnp.concatenate([q_gain, pad]) * (LOG2_E / math.sqrt(MLA_QK)), group).reshape(1, MXU_WIDTH)
    kg = jnp.tile(jnp.concatenate([k_gain, pad]), group).reshape(1, MXU_WIDTH)
    half = MLA_ROPE // 2
    inv = ROPE_THETA ** (-jnp.arange(half, dtype=F32) / half)
    ang = jnp.arange(seq, dtype=F32)[:, None] * inv[None, :]
    zeros = jnp.zeros((seq, half), F32)
    cos = jnp.concatenate([jnp.ones((seq, MLA_NOPE), F32), jnp.cos(ang), jnp.cos(ang),
                           jnp.zeros((seq, LANES - MLA_QK), F32)], axis=1)
    sin_up = jnp.concatenate([jnp.zeros((seq, MLA_NOPE), F32), zeros, jnp.sin(ang),
                              jnp.zeros((seq, LANES - MLA_QK), F32)], axis=1)
    sin_dn = jnp.concatenate([jnp.zeros((seq, MLA_NOPE), F32), -jnp.sin(ang), zeros,
                              jnp.zeros((seq, LANES - MLA_QK), F32)], axis=1)
    cos, sin_up, sin_dn = (jnp.tile(table, (1, group)) for table in (cos, sin_up, sin_dn))
    lane = jnp.arange(MXU_WIDTH)
    ones = (lane[:, None] // LANES == lane[None, :] // LANES).astype(BF16)
    per_seq = seq // tm
    full = lambda a: pl.BlockSpec(a.shape, lambda i: (0,) * a.ndim)
    table = pl.BlockSpec((tm, MXU_WIDTH), lambda i: (i % per_seq, 0))
    consts = (norm_g.reshape(1, d), w_in_p, q_a_gain.reshape(1, -1), kv_a_gain.reshape(1, -1), wq, wk, wv, qg, kg)
    qk_cols = MLA_HEADS * LANES
    v_rows = MLA_HEADS * MLA_V
    slabs = tm // MLA_KEY_TILE
    return pl.pallas_call(
        _mla_proj_kernel,
        grid=(t // tm,),
        in_specs=[pl.BlockSpec((tm, d), lambda i: (i, 0))] + [full(a) for a in consts]
                 + [table, table, table, full(ones)],
        out_specs=[
            pl.BlockSpec((tm, qk_cols), lambda i: (i, 0)),
            pl.BlockSpec((tm, qk_cols), lambda i: (i, 0)),
            pl.BlockSpec((slabs, v_rows, MLA_KEY_TILE), lambda i: (i, 0, 0)),
        ],
        out_shape=[
            jax.ShapeDtypeStruct((t, qk_cols), BF16),
            jax.ShapeDtypeStruct((t, qk_cols), BF16),
            jax.ShapeDtypeStruct((t // MLA_KEY_TILE, v_rows, MLA_KEY_TILE), BF16),
        ],
        scratch_shapes=[pltpu.VMEM((tm, qk_cols), F32), pltpu.VMEM((tm, qk_cols), F32)],
        compiler_params=_params("parallel"),
        name="mla_proj",
    )(h, *consts, cos, sin_up, sin_dn, ones)


def _mla_attn_kernel(q_ref, k_ref, vt_ref, o_ref):
    i = pl.program_id(2)
    heads = MLA_HEADS_PER_STEP
    sub = MLA_KEY_TILE
    bq = q_ref.shape[0]
    n_sub = bq // sub
    lanes = [slice(head * LANES, (head + 1) * LANES) for head in range(heads)]
    qs = [q_ref[:, sl] for sl in lanes]
    key_idx = lax.broadcasted_iota(jnp.int32, (bq, bq), 0)
    qry_idx = lax.broadcasted_iota(jnp.int32, (bq, bq), 1)

    def tile(j, state, diagonal):
        start = pl.multiple_of(j * bq, bq)
        keys = k_ref[pl.ds(start, bq), :]
        sts = [_dot_t(keys[:, lanes[head]], qs[head]) for head in range(heads)]
        if diagonal:
            sts = [jnp.where(key_idx <= qry_idx, st, MASKED) for st in sts]
        ms = [jnp.maximum(state[head][0], jnp.max(sts[head], axis=0, keepdims=True)) for head in range(heads)]
        pts = [jnp.exp2(sts[head] - ms[head]) for head in range(heads)]
        new_state = []
        for head in range(heads):
            m, l, acc = state[head]
            corr = jnp.exp2(m - ms[head])
            pv = sum(jnp.dot(vt_ref[j * n_sub + s, head * MLA_V:(head + 1) * MLA_V, :],
                             pts[head][s * sub:(s + 1) * sub].astype(BF16), preferred_element_type=F32)
                     for s in range(n_sub))
            new_state.append((ms[head], corr * l + jnp.sum(pts[head], axis=0, keepdims=True), corr * acc + pv))
        return tuple(new_state)

    state = tuple((jnp.full((1, bq), MASKED, F32), jnp.zeros((1, bq), F32), jnp.zeros((MLA_V, bq), F32))
                  for _ in range(heads))
    state = tile(i, state, True)
    state = lax.fori_loop(0, i, lambda t, s: tile(t, s, False), state)
    out_t = jnp.concatenate([state[head][2] / state[head][1] for head in range(heads)], axis=0)
    o_ref[...] = out_t.T.astype(BF16)


def _mla_attention(q, k, vt, batch, seq, bq=512):
    heads = MLA_HEADS_PER_STEP
    sub = MLA_KEY_TILE
    q3 = q.reshape(batch, seq, -1)
    k3 = k.reshape(batch, seq, -1)
    vt4 = vt.reshape(batch, seq // sub, MLA_HEADS * MLA_V, sub)
    out = pl.pallas_call(
        _mla_attn_kernel,
        grid=(batch, MLA_HEADS // heads, seq // bq),
        in_specs=[
            pl.BlockSpec((None, bq, heads * LANES), lambda b, p, i: (b, i, p)),
            pl.BlockSpec((None, seq, heads * LANES), lambda b, p, i: (b, 0, p)),
            pl.BlockSpec((None, seq // sub, heads * MLA_V, sub), lambda b, p, i: (b, 0, p, 0)),
        ],
        out_specs=pl.BlockSpec((None, bq, heads * MLA_V), lambda b, p, i: (b, i, p)),
        out_shape=jax.ShapeDtypeStruct((batch, seq, MLA_HEADS * MLA_V), BF16),
        compiler_params=_params("parallel", "parallel", "arbitrary"),
        name="mla_attention",
    )(q3, k3, vt4)
    return out.reshape(batch * seq, MLA_HEADS * MLA_V)


def _sb_layer(h, batch, seq, mix_norm, w_in, q_norm, k_norm, w_out):
    scale = LOG2_E / math.sqrt(SB_HEAD_DIM)
    gains = jnp.stack([jnp.tile(q_norm, SB_HEADS) * scale, jnp.tile(k_norm, SB_HEADS)])
    qkv = _proj_qkv(h, mix_norm, w_in.astype(BF16), gains, SB_HEAD_DIM)
    o = _sb_attention(qkv, batch, seq)
    return _out_proj(o, w_out.astype(BF16), h)


def _dil_layer(h, batch, seq, mix_norm, w_in, q_norm, k_norm, w_out):
    scale = DIL_HEAD_DIM ** -0.5
    w_in = w_in.astype(BF16)
    cols = 3 * D_MODEL
    outs, lses = [], []
    for g, (_, dilation) in enumerate(DIL_GROUPS):
        gains = jnp.stack([jnp.tile(q_norm[g], DIL_HEADS) * scale, jnp.tile(k_norm[g], DIL_HEADS)])
        qkv = _proj_qkv(h, mix_norm, w_in[:, g * cols:(g + 1) * cols], gains, DIL_HEAD_DIM, dilation)
        o, lse = _dil_attention(qkv, g, batch, seq)
        outs.append(o)
        lses.append(lse)
    return _dil_merge_out_proj(outs, lses, w_out.astype(BF16), h)


def _mla_layer(h, batch, seq, mix_norm, w_in, q_a_norm, kv_a_norm, w_q_b, w_kv_b, q_norm, k_norm, w_out):
    q, k, v = _mla_proj(h, mix_norm, w_in, q_a_norm, kv_a_norm, w_q_b, w_kv_b, q_norm, k_norm, seq)
    o = _mla_attention(q, k, v, batch, seq)
    return _out_proj(o, w_out.astype(BF16), h)


def _trunk(x, layers):
    batch, seq, d = x.shape
    h = x.reshape(batch * seq, d)
    mixers = (_sb_layer, _dil_layer, _mla_layer)
    for idx, (mix_norm, mix_params, mlp_norm, w_up, w_down) in enumerate(layers):
        h = mixers[idx % len(mixers)](h, batch, seq, mix_norm, *mix_params)
        h = _mlp(h, mlp_norm, w_up.astype(BF16), w_down.astype(BF16))
    return h.reshape(batch, seq, d)


def kernel(x, l0_mix_norm, l0_sb_w_in, l0_sb_q_norm, l0_sb_k_norm, l0_sb_w_out, l0_mlp_norm, l0_mlp_w_up, l0_mlp_w_down, l1_mix_norm, l1_dil_w_in, l1_dil_q_norm, l1_dil_k_norm, l1_dil_w_out, l1_mlp_norm, l1_mlp_w_up, l1_mlp_w_down, l2_mix_norm, l2_mla_w_in, l2_mla_q_a_norm, l2_mla_kv_a_norm, l2_mla_w_q_b, l2_mla_w_kv_b, l2_mla_q_norm, l2_mla_k_norm, l2_mla_w_out, l2_mlp_norm, l2_mlp_w_up, l2_mlp_w_down, l3_mix_norm, l3_sb_w_in, l3_sb_q_norm, l3_sb_k_norm, l3_sb_w_out, l3_mlp_norm, l3_mlp_w_up, l3_mlp_w_down):
    layers = [
        (l0_mix_norm, (l0_sb_w_in, l0_sb_q_norm, l0_sb_k_norm, l0_sb_w_out),
         l0_mlp_norm, l0_mlp_w_up, l0_mlp_w_down),
        (l1_mix_norm, (l1_dil_w_in, l1_dil_q_norm, l1_dil_k_norm, l1_dil_w_out),
         l1_mlp_norm, l1_mlp_w_up, l1_mlp_w_down),
        (l2_mix_norm, (l2_mla_w_in, l2_mla_q_a_norm, l2_mla_kv_a_norm, l2_mla_w_q_b,
                       l2_mla_w_kv_b, l2_mla_q_norm, l2_mla_k_norm, l2_mla_w_out),
         l2_mlp_norm, l2_mlp_w_up, l2_mlp_w_down),
        (l3_mix_norm, (l3_sb_w_in, l3_sb_q_norm, l3_sb_k_norm, l3_sb_w_out),
         l3_mlp_norm, l3_mlp_w_up, l3_mlp_w_down),
    ]
    return _trunk(x, layers)
```

```python
import functools
import math

import jax
import jax.numpy as jnp
from jax import lax
from jax.experimental import pallas as pl
from jax.experimental.pallas import tpu as pltpu

F32 = jnp.float32
BF16 = jnp.bfloat16

EPS = 1e-6
LANES = 128
MXU_WIDTH = 256
MASKED = -1e30
LOG2_E = math.log2(math.e)
SB_SATURATED = 150.0
VMEM_LIMIT_BYTES = 48 * 1024 * 1024

D_MODEL = 1024
SB_HEADS = 16
SB_HEAD_DIM = 64
DIL_GROUPS = ((128, 1), (512, 4), (2048, 16))
DIL_HEADS = 8
DIL_HEAD_DIM = 128
DIL_BLOCK = 128
DIL_BLOCKS_PER_STEP = 2
MLA_HEADS = 16
MLA_NOPE = 64
MLA_ROPE = 32
MLA_V = 64
MLA_QK = MLA_NOPE + MLA_ROPE
MLA_Q_RANK = 384
MLA_KV_RANK = 256
MLA_KEY_TILE = 256
MLA_HEADS_PER_STEP = 4
MLA_MAX_DENOMINATOR = 2.0 ** 64
ROPE_THETA = 10000.0


def _params(*semantics):
    return pltpu.CompilerParams(dimension_semantics=semantics, vmem_limit_bytes=VMEM_LIMIT_BYTES)


def _rms(x, g):
    ms = jnp.mean(x * x, axis=-1, keepdims=True)
    return x * lax.rsqrt(ms + EPS) * g


def _split_dot(x, w):
    hi = x.astype(BF16)
    lo = (x - hi.astype(F32)).astype(BF16)
    return (jnp.dot(hi, w, preferred_element_type=F32)
            + jnp.dot(lo, w, preferred_element_type=F32))


def _dot_t(a, b):
    return lax.dot_general(a, b, (((1,), (1,)), ((), ())), preferred_element_type=F32)


def _proj_qkv_kernel(x_ref, g_ref, w_ref, gain_ref, seg_ref, o_ref, xn_ref, *y_scratch, head_dim, dilation):
    tm, width = xn_ref.shape[0], D_MODEL
    slabs = width // LANES
    seg_width = seg_ref.shape[0]
    xn_ref[...] = _rms(x_ref[...], g_ref[...]).astype(BF16)

    def project(block, lo):
        col = block * width + lo
        return jnp.dot(xn_ref[...], w_ref[:, col:col + seg_width], preferred_element_type=F32)

    def finish(block, lo, yc):
        if block < 2:
            sumsq = _split_dot(yc * yc, seg_ref[...])
            yc = yc * lax.rsqrt(sumsq + head_dim * EPS) * gain_ref[block:block + 1, lo:lo + seg_width]
        for c in range(seg_width // LANES):
            piece = yc[:, c * LANES:(c + 1) * LANES]
            if dilation == 1:
                o_ref[:, block * width + lo + c * LANES:block * width + lo + (c + 1) * LANES] = piece.astype(BF16)
            else:
                y_scratch[0][block * slabs + lo // LANES + c] = piece

    chunks = [(block, lo) for block in range(3) for lo in range(0, width, seg_width)]
    pending = project(*chunks[0])
    for previous, nxt in zip(chunks[:-1], chunks[1:]):
        following = project(*nxt)
        finish(*previous, pending)
        pending = following
    finish(*chunks[-1], pending)
    if dilation > 1:
        for block in range(3):
            for r in range(dilation):
                for c in range(slabs):
                    rows = y_scratch[0][block * slabs + c, pl.ds(r, tm // dilation, stride=dilation), :]
                    lo = (r * 3 + block) * width + c * LANES
                    o_ref[:, lo:lo + LANES] = rows.astype(BF16)


def _proj_qkv(h, norm_g, w, gains, head_dim, dilation=1, tm=512):
    t, d = h.shape
    width = D_MODEL
    assert w.shape == (d, 3 * width) and tm % (16 * dilation) == 0
    lane = jnp.arange(MXU_WIDTH)
    seg = (lane[:, None] // head_dim == lane[None, :] // head_dim).astype(BF16)
    gains = gains * math.sqrt(head_dim)
    return pl.pallas_call(
        functools.partial(_proj_qkv_kernel, head_dim=head_dim, dilation=dilation),
        grid=(t // tm,),
        in_specs=[
            pl.BlockSpec((tm, d), lambda i: (i, 0)),
            pl.BlockSpec((1, d), lambda i: (0, 0)),
            pl.BlockSpec((d, 3 * width), lambda i: (0, 0)),
            pl.BlockSpec((2, width), lambda i: (0, 0)),
            pl.BlockSpec((MXU_WIDTH, MXU_WIDTH), lambda i: (0, 0)),
        ],
        out_specs=pl.BlockSpec((tm // dilation, dilation * 3 * width), lambda i: (i, 0)),
        out_shape=jax.ShapeDtypeStruct((t // dilation, dilation * 3 * width), BF16),
        scratch_shapes=[pltpu.VMEM((tm, d), BF16)]
                       + ([pltpu.VMEM((3 * width // LANES, tm, LANES), F32)] if dilation > 1 else []),
        compiler_params=_params("parallel"),
        name="proj_qkv_d%d" % dilation,
    )(h, norm_g.reshape(1, d), w, gains, seg)


def _sb_attn_kernel(q_ref, k_ref, v_ref, tri_ref, o_ref, *, blk):
    i = pl.program_id(2)
    lane = lax.broadcasted_iota(jnp.int32, (1, LANES), 1)
    row = lax.broadcasted_iota(jnp.int32, (blk, blk), 0)
    col = lax.broadcasted_iota(jnp.int32, (blk, blk), 1)
    strictly_past = col < row
    per_group = LANES // SB_HEAD_DIM
    groups = q_ref.shape[1] // LANES
    heads = groups * per_group
    group_lanes = [slice((head // per_group) * LANES, (head // per_group + 1) * LANES) for head in range(heads)]
    head_lanes = [(lane // SB_HEAD_DIM) == head % per_group for head in range(heads)]
    qs = [jnp.where(head_lanes[head], q_ref[:, group_lanes[head]], jnp.zeros((blk, LANES), BF16))
          for head in range(heads)]

    def tiles(js, masks, state):
        starts = [pl.multiple_of(jnp.maximum(j, 0) * blk, blk) for j in js]
        pairs = [(t, head) for t in range(len(js)) for head in range(heads)]
        zs = {}
        for t, head in pairs:
            z = _dot_t(qs[head], k_ref[pl.ds(starts[t], blk), group_lanes[head]])
            zs[t, head] = z if masks[t] is None else jnp.where(masks[t], z, MASKED)
        sps = {p: jnp.maximum(zs[p], 0.0) + jnp.log2(1.0 + jnp.exp2(-jnp.abs(zs[p]))) for p in pairs}
        withins = {p: _split_dot(sps[p], tri_ref[...]) for p in pairs}
        later = [state[head][1] for head in range(heads)]
        probs = {}
        for t, head in pairs:
            probs[t, head] = jnp.exp2(zs[t, head] - withins[t, head] - later[head]).astype(BF16)
            later[head] = later[head] + withins[t, head][:, 0:1]
        acc = [state[head][0] for head in range(heads)]
        for t, head in pairs:
            vals = v_ref[pl.ds(starts[t], blk), group_lanes[head]]
            acc[head] = acc[head] + jnp.dot(probs[t, head], vals, preferred_element_type=F32)
        return tuple((acc[head], later[head]) for head in range(heads))

    state = tuple((jnp.zeros((blk, LANES), F32), jnp.zeros((blk, 1), F32)) for _ in range(heads))
    state = tiles([i, i - 1], [strictly_past, i > 0], state)
    remaining = jnp.maximum(i - 1, 0)

    def unsaturated(s):
        lowest = functools.reduce(jnp.minimum, [later for _, later in s])
        return (jnp.min(lowest) < SB_SATURATED).astype(jnp.int32)

    def one_more(carry):
        t, _, s = carry
        s = tiles([i - 2 - t], [None], s)
        return t + 1, unsaturated(s), s

    _, _, state = lax.while_loop(lambda c: jnp.logical_and(c[0] < remaining, c[1] > 0), one_more,
                                 (jnp.int32(0), unsaturated(state), state))
    for group in range(groups):
        members = range(group * per_group, (group + 1) * per_group)
        out = sum(jnp.where(head_lanes[head], state[head][0], 0.0) for head in members)
        o_ref[:, group * LANES:(group + 1) * LANES] = out.astype(BF16)


def _sb_attention(qkv, batch, seq, blk=256, groups=2):
    width = groups * LANES
    steps = D_MODEL // width
    nq = seq // blk
    qkv3 = qkv.reshape(batch, seq, 3 * D_MODEL)
    idx = jnp.arange(blk)
    tri = (idx[:, None] >= idx[None, :]).astype(BF16)
    out = pl.pallas_call(
        functools.partial(_sb_attn_kernel, blk=blk),
        grid=(batch, steps, nq),
        in_specs=[
            pl.BlockSpec((None, blk, width), lambda b, p, i: (b, i, p)),
            pl.BlockSpec((None, seq, width), lambda b, p, i: (b, 0, steps + p)),
            pl.BlockSpec((None, seq, width), lambda b, p, i: (b, 0, 2 * steps + p)),
            pl.BlockSpec((blk, blk), lambda b, p, i: (0, 0)),
        ],
        out_specs=pl.BlockSpec((None, blk, width), lambda b, p, i: (b, i, p)),
        out_shape=jax.ShapeDtypeStruct((batch, seq, D_MODEL), BF16),
        compiler_params=_params("parallel", "parallel", "arbitrary"),
        name="sb_attention",
    )(qkv3, qkv3, qkv3, tri)
    return out.reshape(batch * seq, D_MODEL)


def _out_proj_kernel(o_ref, w_ref, h_ref, out_ref):
    out_ref[...] = h_ref[...] + jnp.dot(o_ref[...], w_ref[...], preferred_element_type=F32)


def _out_proj(o, w, h, tm=1024):
    t, d = h.shape
    return pl.pallas_call(
        _out_proj_kernel,
        grid=(t // tm,),
        in_specs=[
            pl.BlockSpec((tm, o.shape[1]), lambda i: (i, 0)),
            pl.BlockSpec(w.shape, lambda i: (0, 0)),
            pl.BlockSpec((tm, d), lambda i: (i, 0)),
        ],
        out_specs=pl.BlockSpec((tm, d), lambda i: (i, 0)),
        out_shape=jax.ShapeDtypeStruct((t, d), F32),
        compiler_params=_params("parallel"),
        name="out_proj",
    )(o, w, h)


def _mlp_kernel(h_ref, g_ref, wu_ref, wd_ref, out_ref, xn_ref, acc_ref):
    f = pl.program_id(1)

    @pl.when(f == 0)
    def _():
        xn_ref[...] = _rms(h_ref[...], g_ref[...]).astype(BF16)
        acc_ref[...] = jnp.zeros_like(acc_ref)

    u = jnp.maximum(jnp.dot(xn_ref[...], wu_ref[...], preferred_element_type=F32), 0.0)
    acc_ref[...] += jnp.dot((u * u).astype(BF16), wd_ref[...], preferred_element_type=F32)

    @pl.when(f == pl.num_programs(1) - 1)
    def _():
        out_ref[...] = h_ref[...] + acc_ref[...]


def _mlp(h, norm_g, w_up, w_down, tm=1024, fc=1024):
    t, d = h.shape
    ff = w_up.shape[1]
    return pl.pallas_call(
        _mlp_kernel,
        grid=(t // tm, ff // fc),
        in_specs=[
            pl.BlockSpec((tm, d), lambda i, f: (i, 0)),
            pl.BlockSpec((1, d), lambda i, f: (0, 0)),
            pl.BlockSpec((d, fc), lambda i, f: (0, f)),
            pl.BlockSpec((fc, d), lambda i, f: (f, 0)),
        ],
        out_specs=pl.BlockSpec((tm, d), lambda i, f: (i, 0)),
        out_shape=jax.ShapeDtypeStruct((t, d), F32),
        scratch_shapes=[pltpu.VMEM((tm, d), BF16), pltpu.VMEM((tm, d), F32)],
        compiler_params=_params("parallel", "arbitrary"),
        name="mlp",
    )(h, norm_g.reshape(1, d), w_up, w_down)


def _dil_attn_kernel(q_ref, kc_ref, kp_ref, vc_ref, vp_ref, o_ref, lse_ref, *, bias_per_step, max_steps):
    n = pl.program_id(2)
    blk = DIL_BLOCK
    qi = lax.broadcasted_iota(jnp.int32, (blk, 2 * blk), 0)
    kj = lax.broadcasted_iota(jnp.int32, (blk, 2 * blk), 1)
    steps = blk + qi - kj
    in_window = (steps >= 0) & (steps <= max_steps)
    steps_f = steps.astype(F32)
    lane = lax.broadcasted_iota(jnp.int32, (1, LANES), 1)
    lanes = [slice(head * DIL_HEAD_DIM, (head + 1) * DIL_HEAD_DIM) for head in range(DIL_HEADS)]

    def window(prev_ref, cur_ref, sub, sl):
        if sub == 0:
            return jnp.concatenate([prev_ref[:, sl], cur_ref[0:blk, sl]], axis=0)
        return cur_ref[(sub - 1) * blk:(sub + 1) * blk, sl]

    for sub in range(q_ref.shape[0] // blk):
        rows = slice(sub * blk, (sub + 1) * blk)
        valid = in_window & ((kj >= blk) | (n > 0)) if sub == 0 else in_window
        zs = [_dot_t(q_ref[rows, sl], window(kp_ref, kc_ref, sub, sl)) for sl in lanes]
        zs = [jnp.where(valid, z - bias_per_step[head] * steps_f, MASKED) for head, z in enumerate(zs)]
        ms = [jnp.max(z, axis=-1, keepdims=True) for z in zs]
        ps = [jnp.exp(z - m) for z, m in zip(zs, ms)]
        denoms = [jnp.sum(p, axis=-1, keepdims=True) for p in ps]
        pvs = [jnp.dot(p.astype(BF16), window(vp_ref, vc_ref, sub, sl), preferred_element_type=F32)
               for p, sl in zip(ps, lanes)]
        lse_tile = jnp.zeros((blk, LANES), F32)
        for head, sl in enumerate(lanes):
            o_ref[rows, sl] = pvs[head] / denoms[head]
            lse_tile = jnp.where(lane == head, ms[head] + jnp.log(denoms[head]), lse_tile)
        lse_ref[rows, :] = lse_tile


def _dil_attention(qkv, group, batch, seq):
    window, dilation = DIL_GROUPS[group]
    n_groups = len(DIL_GROUPS)
    blk = DIL_BLOCK
    length = seq // dilation
    assert length % blk == 0
    nb = length // blk
    width = DIL_HEADS * DIL_HEAD_DIM
    slopes = [2.0 ** (-8.0 * (group * DIL_HEADS + hd + 1) / (n_groups * DIL_HEADS)) for hd in range(DIL_HEADS)]
    view = qkv.reshape(batch, length, dilation * 3 * width)

    subs = DIL_BLOCKS_PER_STEP
    assert nb % subs == 0
    step_rows = subs * blk

    def col(offset):
        return lambda b, r, n: (b, n, r * 3 + offset)

    def col_prev(offset):
        return lambda b, r, n: (b, jnp.maximum(n * subs - 1, 0), r * 3 + offset)

    o, lse = pl.pallas_call(
        functools.partial(_dil_attn_kernel,
                          bias_per_step=tuple(s * dilation for s in slopes),
                          max_steps=window // dilation),
        grid=(batch, dilation, nb // subs),
        in_specs=[
            pl.BlockSpec((None, step_rows, width), col(0)),
            pl.BlockSpec((None, step_rows, width), col(1)),
            pl.BlockSpec((None, blk, width), col_prev(1)),
            pl.BlockSpec((None, step_rows, width), col(2)),
            pl.BlockSpec((None, blk, width), col_prev(2)),
        ],
        out_specs=[
            pl.BlockSpec((None, step_rows, width), lambda b, r, n: (b, n, r)),
            pl.BlockSpec((None, step_rows, LANES), lambda b, r, n: (b, n, r)),
        ],
        out_shape=[
            jax.ShapeDtypeStruct((batch, length, dilation * width), F32),
            jax.ShapeDtypeStruct((batch, length, dilation * LANES), F32),
        ],
        compiler_params=_params("parallel", "parallel", "arbitrary"),
        name="dil_attention_g%d" % group,
    )(view, view, view, view, view)
    rows = batch * length
    return o.reshape(rows, dilation * width), lse.reshape(rows, dilation * LANES)


def _dil_merge_kernel(o0_ref, o1_ref, o2_ref, l0_ref, l1_ref, l2_ref, w_ref, h_ref, out_ref,
                      o_tok_ref, l_tok_ref, mrg_ref):
    tm = h_ref.shape[0]
    slabs = D_MODEL // LANES
    o_in = (o0_ref, o1_ref, o2_ref)
    l_in = (l0_ref, l1_ref, l2_ref)
    for g, (_, dilation) in enumerate(DIL_GROUPS):
        for r in range(dilation):
            rows = slice(None) if dilation == 1 else pl.ds(r, tm // dilation, stride=dilation)
            l_tok_ref[g, rows, :] = l_in[g][:, r * LANES:(r + 1) * LANES]
            for c in range(slabs):
                lo = r * D_MODEL + c * LANES
                o_tok_ref[g * slabs + c, rows, :] = o_in[g][:, lo:lo + LANES]
    lses = [l_tok_ref[g] for g in range(len(DIL_GROUPS))]
    m = jnp.maximum(jnp.maximum(lses[0], lses[1]), lses[2])
    es = [jnp.exp(l - m) for l in lses]
    total = es[0] + es[1] + es[2]
    alphas = [e / total for e in es]
    for head in range(DIL_HEADS):
        merged = sum(alphas[g][:, head:head + 1] * o_tok_ref[g * slabs + head] for g in range(len(DIL_GROUPS)))
        mrg_ref[:, head * DIL_HEAD_DIM:(head + 1) * DIL_HEAD_DIM] = merged.astype(BF16)
    out_ref[...] = h_ref[...] + jnp.dot(mrg_ref[...], w_ref[...], preferred_element_type=F32)


def _dil_merge_out_proj(outs, lses, w, h, tm=512):
    t, d = h.shape
    n_groups = len(DIL_GROUPS)
    grouped = lambda cols, dil: pl.BlockSpec((tm // dil, dil * cols), lambda i: (i, 0))
    dils = [dil for _, dil in DIL_GROUPS]
    return pl.pallas_call(
        _dil_merge_kernel,
        grid=(t // tm,),
        in_specs=[grouped(d, dil) for dil in dils] + [grouped(LANES, dil) for dil in dils]
                 + [pl.BlockSpec(w.shape, lambda i: (0, 0)), pl.BlockSpec((tm, d), lambda i: (i, 0))],
        out_specs=pl.BlockSpec((tm, d), lambda i: (i, 0)),
        out_shape=jax.ShapeDtypeStruct((t, d), F32),
        scratch_shapes=[pltpu.VMEM((n_groups * d // LANES, tm, LANES), F32),
                        pltpu.VMEM((n_groups, tm, LANES), F32),
                        pltpu.VMEM((tm, d), BF16)],
        compiler_params=_params("parallel"),
        name="dil_merge_out_proj",
    )(*outs, *lses, w, h)


def _mla_proj_kernel(x_ref, g_ref, win_ref, qa_ref, kva_ref, wq_ref, wk_ref, wv_ref, qg_ref, kg_ref,
                     cos_ref, sin_up_ref, sin_dn_ref, ones_ref, q_out, k_out, v_out, qp_ref, kp_ref):
    xn = _rms(x_ref[...], g_ref[...]).astype(BF16)
    c = jnp.dot(xn, win_ref[...], preferred_element_type=F32)
    cq = _rms(c[:, :MLA_Q_RANK], qa_ref[...]).astype(BF16)
    ckv = _rms(c[:, MLA_Q_RANK:MLA_Q_RANK + MLA_KV_RANK], kva_ref[...]).astype(BF16)
    shared_rope = c[:, MLA_Q_RANK + MLA_KV_RANK:]
    qp_ref[...] = jnp.dot(cq, wq_ref[...], preferred_element_type=F32)
    kp_ref[...] = jnp.dot(ckv, wk_ref[...], preferred_element_type=F32)
    v = jnp.dot(ckv, wv_ref[...], preferred_element_type=F32)
    for s in range(v_out.shape[0]):
        v_out[s] = v[s * MLA_KEY_TILE:(s + 1) * MLA_KEY_TILE, :].T.astype(BF16)
    cos = cos_ref[...]
    sin_up = sin_up_ref[...]
    sin_dn = sin_dn_ref[...]
    half = MLA_ROPE // 2

    group = ones_ref.shape[0]

    def norm_rope(x, gain):
        y = x * lax.rsqrt(_split_dot(x * x, ones_ref[...]) + MLA_QK * EPS) * gain
        return y * cos + pltpu.roll(y, half, 1) * sin_up + pltpu.roll(y, group - half, 1) * sin_dn

    shared = jnp.concatenate([shared_rope] * (group // LANES), axis=1)
    for lo in range(0, MLA_HEADS * LANES, group):
        sl = slice(lo, lo + group)
        q_out[:, sl] = norm_rope(qp_ref[:, sl], qg_ref[...]).astype(BF16)
        k_out[:, sl] = norm_rope(kp_ref[:, sl] + shared, kg_ref[...]).astype(BF16)


def _pad_heads(w, heads, src_lo, src_hi, src_width):
    k = w.shape[0]
    w = w.reshape(k, heads, src_width)[:, :, src_lo:src_hi]
    w = jnp.pad(w, ((0, 0), (0, 0), (0, LANES - (src_hi - src_lo))))
    return w.reshape(k, heads * LANES)


def _mla_proj(h, norm_g, w_in, q_a_gain, kv_a_gain, w_q_b, w_kv_b, q_gain, k_gain, seq, tm=512):
    t, d = h.shape
    latent = MLA_Q_RANK + MLA_KV_RANK
    w_in_p = jnp.concatenate([
        w_in[:, :latent],
        jnp.zeros((d, MLA_NOPE), w_in.dtype),
        w_in[:, latent:],
        jnp.zeros((d, LANES - MLA_QK), w_in.dtype)], axis=1).astype(BF16)
    wq = _pad_heads(w_q_b, MLA_HEADS, 0, MLA_QK, MLA_QK).astype(BF16)
    wk = _pad_heads(w_kv_b, MLA_HEADS, 0, MLA_NOPE, MLA_NOPE + MLA_V).astype(BF16)
    wv = w_kv_b.reshape(MLA_KV_RANK, MLA_HEADS, MLA_NOPE + MLA_V)[:, :, MLA_NOPE:]
    wv = wv.reshape(MLA_KV_RANK, MLA_HEADS * MLA_V).astype(BF16)
    pad = jnp.zeros((LANES - MLA_QK,), F32)
    group = MXU_WIDTH // LANES
    qg = jnp.tile(jnp.concatenate([q_gain, pad]) * LOG2_E, group).reshape(1, MXU_WIDTH)
    kg = jnp.tile(jnp.concatenate([k_gain, pad]) * math.sqrt(MLA_QK), group).reshape(1, MXU_WIDTH)
    half = MLA_ROPE // 2
    inv = ROPE_THETA ** (-jnp.arange(half, dtype=F32) / half)
    ang = jnp.arange(seq, dtype=F32)[:, None] * inv[None, :]
    zeros = jnp.zeros((seq, half), F32)
    cos = jnp.concatenate([jnp.ones((seq, MLA_NOPE), F32), jnp.cos(ang), jnp.cos(ang),
                           jnp.zeros((seq, LANES - MLA_QK), F32)], axis=1)
    sin_up = jnp.concatenate([jnp.zeros((seq, MLA_NOPE), F32), zeros, jnp.sin(ang),
                              jnp.zeros((seq, LANES - MLA_QK), F32)], axis=1)
    sin_dn = jnp.concatenate([jnp.zeros((seq, MLA_NOPE), F32), -jnp.sin(ang), zeros,
                              jnp.zeros((seq, LANES - MLA_QK), F32)], axis=1)
    cos, sin_up, sin_dn = (jnp.tile(table, (1, group)) for table in (cos, sin_up, sin_dn))
    lane = jnp.arange(MXU_WIDTH)
    ones = (lane[:, None] // LANES == lane[None, :] // LANES).astype(BF16)
    per_seq = seq // tm
    full = lambda a: pl.BlockSpec(a.shape, lambda i: (0,) * a.ndim)
    table = pl.BlockSpec((tm, MXU_WIDTH), lambda i: (i % per_seq, 0))
    consts = (norm_g.reshape(1, d), w_in_p, q_a_gain.reshape(1, -1), kv_a_gain.reshape(1, -1), wq, wk, wv, qg, kg)
    qk_cols = MLA_HEADS * LANES
    v_rows = MLA_HEADS * MLA_V
    slabs = tm // MLA_KEY_TILE
    return pl.pallas_call(
        _mla_proj_kernel,
        grid=(t // tm,),
        in_specs=[pl.BlockSpec((tm, d), lambda i: (i, 0))] + [full(a) for a in consts]
                 + [table, table, table, full(ones)],
        out_specs=[
            pl.BlockSpec((tm, qk_cols), lambda i: (i, 0)),
            pl.BlockSpec((tm, qk_cols), lambda i: (i, 0)),
            pl.BlockSpec((slabs, v_rows, MLA_KEY_TILE), lambda i: (i, 0, 0)),
        ],
        out_shape=[
            jax.ShapeDtypeStruct((t, qk_cols), BF16),
            jax.ShapeDtypeStruct((t, qk_cols), BF16),
            jax.ShapeDtypeStruct((t // MLA_KEY_TILE, v_rows, MLA_KEY_TILE), BF16),
        ],
        scratch_shapes=[pltpu.VMEM((tm, qk_cols), F32), pltpu.VMEM((tm, qk_cols), F32)],
        compiler_params=_params("parallel"),
        name="mla_proj",
    )(h, *consts, cos, sin_up, sin_dn, ones)


def _mla_attn_kernel(q_ref, k_ref, vt_ref, o_ref):
    i = pl.program_id(2)
    heads = MLA_HEADS_PER_STEP
    sub = MLA_KEY_TILE
    bq = q_ref.shape[0]
    n_sub = bq // sub
    lanes = [slice(head * LANES, (head + 1) * LANES) for head in range(heads)]
    qs = [q_ref[:, sl] for sl in lanes]
    key_idx = lax.broadcasted_iota(jnp.int32, (bq, bq), 0)
    qry_idx = lax.broadcasted_iota(jnp.int32, (bq, bq), 1)

    def tile(j, state, diagonal):
        start = pl.multiple_of(j * bq, bq)
        keys = k_ref[pl.ds(start, bq), :]
        sts = [_dot_t(keys[:, lanes[head]], qs[head]) for head in range(heads)]
        if diagonal:
            sts = [jnp.where(key_idx <= qry_idx, st, MASKED) for st in sts]
        ms = [jnp.maximum(state[head][0], jnp.max(sts[head], axis=0, keepdims=True)) for head in range(heads)]
        pts = [jnp.exp2(sts[head] - ms[head]) for head in range(heads)]
        new_state = []
        for head in range(heads):
            m, l, acc = state[head]
            corr = jnp.exp2(m - ms[head])
            pv = sum(jnp.dot(vt_ref[j * n_sub + s, head * MLA_V:(head + 1) * MLA_V, :],
                             pts[head][s * sub:(s + 1) * sub].astype(BF16), preferred_element_type=F32)
                     for s in range(n_sub))
            new_state.append((ms[head], corr * l + jnp.sum(pts[head], axis=0, keepdims=True), corr * acc + pv))
        return tuple(new_state)

    def fixed_reference_tile(j, state):
        start = pl.multiple_of(j * bq, bq)
        keys = k_ref[pl.ds(start, bq), :]
        sts = [_dot_t(keys[:, lanes[head]], qs[head]) for head in range(heads)]
        pts = [jnp.exp2(sts[head] - state[head][0]) for head in range(heads)]
        new_state = []
        for head in range(heads):
            m, l, acc = state[head]
            pv = sum(jnp.dot(vt_ref[j * n_sub + s, head * MLA_V:(head + 1) * MLA_V, :],
                             pts[head][s * sub:(s + 1) * sub].astype(BF16), preferred_element_type=F32)
                     for s in range(n_sub))
            new_state.append((m, l + jnp.sum(pts[head], axis=0, keepdims=True), acc + pv))
        return tuple(new_state)

    init = tuple((jnp.full((1, bq), MASKED, F32), jnp.zeros((1, bq), F32), jnp.zeros((MLA_V, bq), F32))
                 for _ in range(heads))
    diag = tile(i, init, True)
    fast = lax.fori_loop(0, i, lambda t, s: fixed_reference_tile(t, s), diag)
    largest = functools.reduce(jnp.maximum, [l for _, l, _ in fast])
    in_range = jnp.max(largest) < MLA_MAX_DENOMINATOR

    def finish(state):
        out_t = jnp.concatenate([acc / l for _, l, acc in state], axis=0)
        return out_t.T.astype(BF16)

    o_ref[...] = lax.cond(in_range, lambda: finish(fast),
                          lambda: finish(lax.fori_loop(0, i, lambda t, s: tile(t, s, False), diag)))


def _mla_attention(q, k, vt, batch, seq, bq=512):
    heads = MLA_HEADS_PER_STEP
    sub = MLA_KEY_TILE
    q3 = q.reshape(batch, seq, -1)
    k3 = k.reshape(batch, seq, -1)
    vt4 = vt.reshape(batch, seq // sub, MLA_HEADS * MLA_V, sub)
    out = pl.pallas_call(
        _mla_attn_kernel,
        grid=(batch, MLA_HEADS // heads, seq // bq),
        in_specs=[
            pl.BlockSpec((None, bq, heads * LANES), lambda b, p, i: (b, i, p)),
            pl.BlockSpec((None, seq, heads * LANES), lambda b, p, i: (b, 0, p)),
            pl.BlockSpec((None, seq // sub, heads * MLA_V, sub), lambda b, p, i: (b, 0, p, 0)),
        ],
        out_specs=pl.BlockSpec((None, bq, heads * MLA_V), lambda b, p, i: (b, i, p)),
        out_shape=jax.ShapeDtypeStruct((batch, seq, MLA_HEADS * MLA_V), BF16),
        compiler_params=_params("parallel", "parallel", "arbitrary"),
        name="mla_attention",
    )(q3, k3, vt4)
    return out.reshape(batch * seq, MLA_HEADS * MLA_V)


def _sb_layer(h, batch, seq, mix_norm, w_in, q_norm, k_norm, w_out):
    scale = LOG2_E / math.sqrt(SB_HEAD_DIM)
    gains = jnp.stack([jnp.tile(q_norm, SB_HEADS) * scale, jnp.tile(k_norm, SB_HEADS)])
    qkv = _proj_qkv(h, mix_norm, w_in.astype(BF16), gains, SB_HEAD_DIM)
    o = _sb_attention(qkv, batch, seq)
    return _out_proj(o, w_out.astype(BF16), h)


def _dil_layer(h, batch, seq, mix_norm, w_in, q_norm, k_norm, w_out):
    scale = DIL_HEAD_DIM ** -0.5
    w_in = w_in.astype(BF16)
    cols = 3 * D_MODEL
    outs, lses = [], []
    for g, (_, dilation) in enumerate(DIL_GROUPS):
        gains = jnp.stack([jnp.tile(q_norm[g], DIL_HEADS) * scale, jnp.tile(k_norm[g], DIL_HEADS)])
        qkv = _proj_qkv(h, mix_norm, w_in[:, g * cols:(g + 1) * cols], gains, DIL_HEAD_DIM, dilation)
        o, lse = _dil_attention(qkv, g, batch, seq)
        outs.append(o)
        lses.append(lse)
    return _dil_merge_out_proj(outs, lses, w_out.astype(BF16), h)


def _mla_layer(h, batch, seq, mix_norm, w_in, q_a_norm, kv_a_norm, w_q_b, w_kv_b, q_norm, k_norm, w_out):
    q, k, v = _mla_proj(h, mix_norm, w_in, q_a_norm, kv_a_norm, w_q_b, w_kv_b, q_norm, k_norm, seq)
    o = _mla_attention(q, k, v, batch, seq)
    return _out_proj(o, w_out.astype(BF16), h)


def _trunk(x, layers):
    batch, seq, d = x.shape
    h = x.reshape(batch * seq, d)
    mixers = (_sb_layer, _dil_layer, _mla_layer)
    for idx, (mix_norm, mix_params, mlp_norm, w_up, w_down) in enumerate(layers):
        h = mixers[idx % len(mixers)](h, batch, seq, mix_norm, *mix_params)
        h = _mlp(h, mlp_norm, w_up.astype(BF16), w_down.astype(BF16))
    return h.reshape(batch, seq, d)


def kernel(x, l0_mix_norm, l0_sb_w_in, l0_sb_q_norm, l0_sb_k_norm, l0_sb_w_out, l0_mlp_norm, l0_mlp_w_up, l0_mlp_w_down, l1_mix_norm, l1_dil_w_in, l1_dil_q_norm, l1_dil_k_norm, l1_dil_w_out, l1_mlp_norm, l1_mlp_w_up, l1_mlp_w_down, l2_mix_norm, l2_mla_w_in, l2_mla_q_a_norm, l2_mla_kv_a_norm, l2_mla_w_q_b, l2_mla_w_kv_b, l2_mla_q_norm, l2_mla_k_norm, l2_mla_w_out, l2_mlp_norm, l2_mlp_w_up, l2_mlp_w_down, l3_mix_norm, l3_sb_w_in, l3_sb_q_norm, l3_sb_k_norm, l3_sb_w_out, l3_mlp_norm, l3_mlp_w_up, l3_mlp_w_down):
    layers = [
        (l0_mix_norm, (l0_sb_w_in, l0_sb_q_norm, l0_sb_k_norm, l0_sb_w_out),
         l0_mlp_norm, l0_mlp_w_up, l0_mlp_w_down),
        (l1_mix_norm, (l1_dil_w_in, l1_dil_q_norm, l1_dil_k_norm, l1_dil_w_out),
         l1_mlp_norm, l1_mlp_w_up, l1_mlp_w_down),
        (l2_mix_norm, (l2_mla_w_in, l2_mla_q_a_norm, l2_mla_kv_a_norm, l2_mla_w_q_b,
                       l2_mla_w_kv_b, l2_mla_q_norm, l2_mla_k_norm, l2_mla_w_out),
         l2_mlp_norm, l2_mlp_w_up, l2_mlp_w_down),
        (l3_mix_norm, (l3_sb_w_in, l3_sb_q_norm, l3_sb_k_norm, l3_sb_w_out),
         l3_mlp_norm, l3_mlp_w_up, l3_mlp_w_down),
    ]
    return _trunk(x, layers)
```

```python
import functools
import math

import jax
import jax.numpy as jnp
from jax import lax
from jax.experimental import pallas as pl
from jax.experimental.pallas import tpu as pltpu

F32 = jnp.float32
BF16 = jnp.bfloat16

EPS = 1e-6
LANES = 128
MXU_WIDTH = 256
MASKED = -1e30
LOG2_E = math.log2(math.e)
SB_SATURATED = 150.0
SB_LINEAR = 30.0
VMEM_LIMIT_BYTES = 48 * 1024 * 1024

D_MODEL = 1024
SB_HEADS = 16
SB_HEAD_DIM = 64
DIL_GROUPS = ((128, 1), (512, 4), (2048, 16))
DIL_HEADS = 8
DIL_HEAD_DIM = 128
DIL_BLOCK = 128
DIL_BLOCKS_PER_STEP = 2
MLA_HEADS = 16
MLA_NOPE = 64
MLA_ROPE = 32
MLA_V = 64
MLA_QK = MLA_NOPE + MLA_ROPE
MLA_Q_RANK = 384
MLA_KV_RANK = 256
MLA_KEY_TILE = 256
MLA_HEADS_PER_STEP = 4
MLA_MAX_DENOMINATOR = 2.0 ** 64
ROPE_THETA = 10000.0


def _params(*semantics):
    return pltpu.CompilerParams(dimension_semantics=semantics, vmem_limit_bytes=VMEM_LIMIT_BYTES)


def _rms(x, g):
    ms = jnp.mean(x * x, axis=-1, keepdims=True)
    return x * lax.rsqrt(ms + EPS) * g


def _split_dot(x, w):
    hi = x.astype(BF16)
    lo = (x - hi.astype(F32)).astype(BF16)
    return (jnp.dot(hi, w, preferred_element_type=F32)
            + jnp.dot(lo, w, preferred_element_type=F32))


def _dot_t(a, b):
    return lax.dot_general(a, b, (((1,), (1,)), ((), ())), preferred_element_type=F32)


def _proj_qkv_kernel(x_ref, g_ref, w_ref, gain_ref, seg_ref, o_ref, xn_ref, *y_scratch, head_dim, dilation):
    tm, width = xn_ref.shape[0], D_MODEL
    slabs = width // LANES
    seg_width = seg_ref.shape[0]
    xn_ref[...] = _rms(x_ref[...], g_ref[...]).astype(BF16)

    def project(block, lo):
        col = block * width + lo
        return jnp.dot(xn_ref[...], w_ref[:, col:col + seg_width], preferred_element_type=F32)

    def finish(block, lo, yc):
        if block < 2:
            sumsq = _split_dot(yc * yc, seg_ref[...])
            yc = yc * lax.rsqrt(sumsq + head_dim * EPS) * gain_ref[block:block + 1, lo:lo + seg_width]
        for c in range(seg_width // LANES):
            piece = yc[:, c * LANES:(c + 1) * LANES]
            if dilation == 1:
                o_ref[:, block * width + lo + c * LANES:block * width + lo + (c + 1) * LANES] = piece.astype(BF16)
            else:
                y_scratch[0][block * slabs + lo // LANES + c] = piece

    chunks = [(block, lo) for block in range(3) for lo in range(0, width, seg_width)]
    pending = project(*chunks[0])
    for previous, nxt in zip(chunks[:-1], chunks[1:]):
        following = project(*nxt)
        finish(*previous, pending)
        pending = following
    finish(*chunks[-1], pending)
    if dilation > 1:
        for block in range(3):
            for r in range(dilation):
                for c in range(slabs):
                    rows = y_scratch[0][block * slabs + c, pl.ds(r, tm // dilation, stride=dilation), :]
                    lo = (r * 3 + block) * width + c * LANES
                    o_ref[:, lo:lo + LANES] = rows.astype(BF16)


def _proj_qkv(h, norm_g, w, gains, head_dim, dilation=1, tm=512):
    t, d = h.shape
    width = D_MODEL
    assert w.shape == (d, 3 * width) and tm % (16 * dilation) == 0
    lane = jnp.arange(MXU_WIDTH)
    seg = (lane[:, None] // head_dim == lane[None, :] // head_dim).astype(BF16)
    gains = gains * math.sqrt(head_dim)
    return pl.pallas_call(
        functools.partial(_proj_qkv_kernel, head_dim=head_dim, dilation=dilation),
        grid=(t // tm,),
        in_specs=[
            pl.BlockSpec((tm, d), lambda i: (i, 0)),
            pl.BlockSpec((1, d), lambda i: (0, 0)),
            pl.BlockSpec((d, 3 * width), lambda i: (0, 0)),
            pl.BlockSpec((2, width), lambda i: (0, 0)),
            pl.BlockSpec((MXU_WIDTH, MXU_WIDTH), lambda i: (0, 0)),
        ],
        out_specs=pl.BlockSpec((tm // dilation, dilation * 3 * width), lambda i: (i, 0)),
        out_shape=jax.ShapeDtypeStruct((t // dilation, dilation * 3 * width), BF16),
        scratch_shapes=[pltpu.VMEM((tm, d), BF16)]
                       + ([pltpu.VMEM((3 * width // LANES, tm, LANES), F32)] if dilation > 1 else []),
        compiler_params=_params("parallel"),
        name="proj_qkv_d%d" % dilation,
    )(h, norm_g.reshape(1, d), w, gains, seg)


def _sb_attn_kernel(q_ref, k_ref, v_ref, tri_ref, o_ref, *, blk):
    i = pl.program_id(2)
    lane = lax.broadcasted_iota(jnp.int32, (1, LANES), 1)
    row = lax.broadcasted_iota(jnp.int32, (blk, blk), 0)
    col = lax.broadcasted_iota(jnp.int32, (blk, blk), 1)
    strictly_past = col < row
    per_group = LANES // SB_HEAD_DIM
    groups = q_ref.shape[1] // LANES
    heads = groups * per_group
    group_lanes = [slice((head // per_group) * LANES, (head // per_group + 1) * LANES) for head in range(heads)]
    head_lanes = [(lane // SB_HEAD_DIM) == head % per_group for head in range(heads)]
    qs = [jnp.where(head_lanes[head], q_ref[:, group_lanes[head]], jnp.zeros((blk, LANES), BF16))
          for head in range(heads)]

    def tiles(js, masks, state):
        starts = [pl.multiple_of(jnp.maximum(j, 0) * blk, blk) for j in js]
        pairs = [(t, head) for t in range(len(js)) for head in range(heads)]
        zs = {}
        for t, head in pairs:
            z = _dot_t(qs[head], k_ref[pl.ds(starts[t], blk), group_lanes[head]])
            zs[t, head] = z if masks[t] is None else jnp.where(masks[t], z, MASKED)
        sps = {p: jnp.where(zs[p] > SB_LINEAR, zs[p], jnp.log2(1.0 + jnp.exp2(zs[p]))) for p in pairs}
        withins = {p: _split_dot(sps[p], tri_ref[...]) for p in pairs}
        later = [state[head][1] for head in range(heads)]
        probs = {}
        for t, head in pairs:
            probs[t, head] = jnp.exp2(zs[t, head] - withins[t, head] - later[head]).astype(BF16)
            later[head] = later[head] + withins[t, head][:, 0:1]
        acc = [state[head][0] for head in range(heads)]
        for t, head in pairs:
            vals = v_ref[pl.ds(starts[t], blk), group_lanes[head]]
            acc[head] = acc[head] + jnp.dot(probs[t, head], vals, preferred_element_type=F32)
        return tuple((acc[head], later[head]) for head in range(heads))

    state = tuple((jnp.zeros((blk, LANES), F32), jnp.zeros((blk, 1), F32)) for _ in range(heads))
    state = tiles([i, i - 1], [strictly_past, i > 0], state)
    remaining = jnp.maximum(i - 1, 0)

    def unsaturated(s):
        lowest = functools.reduce(jnp.minimum, [later for _, later in s])
        return (jnp.min(lowest) < SB_SATURATED).astype(jnp.int32)

    def one_more(carry):
        t, _, s = carry
        s = tiles([i - 2 - t], [None], s)
        return t + 1, unsaturated(s), s

    _, _, state = lax.while_loop(lambda c: jnp.logical_and(c[0] < remaining, c[1] > 0), one_more,
                                 (jnp.int32(0), unsaturated(state), state))
    for group in range(groups):
        members = range(group * per_group, (group + 1) * per_group)
        out = sum(jnp.where(head_lanes[head], state[head][0], 0.0) for head in members)
        o_ref[:, group * LANES:(group + 1) * LANES] = out.astype(BF16)


def _sb_attention(qkv, batch, seq, blk=256, groups=2):
    width = groups * LANES
    steps = D_MODEL // width
    nq = seq // blk
    qkv3 = qkv.reshape(batch, seq, 3 * D_MODEL)
    idx = jnp.arange(blk)
    tri = (idx[:, None] >= idx[None, :]).astype(BF16)
    out = pl.pallas_call(
        functools.partial(_sb_attn_kernel, blk=blk),
        grid=(batch, steps, nq),
        in_specs=[
            pl.BlockSpec((None, blk, width), lambda b, p, i: (b, i, p)),
            pl.BlockSpec((None, seq, width), lambda b, p, i: (b, 0, steps + p)),
            pl.BlockSpec((None, seq, width), lambda b, p, i: (b, 0, 2 * steps + p)),
            pl.BlockSpec((blk, blk), lambda b, p, i: (0, 0)),
        ],
        out_specs=pl.BlockSpec((None, blk, width), lambda b, p, i: (b, i, p)),
        out_shape=jax.ShapeDtypeStruct((batch, seq, D_MODEL), BF16),
        compiler_params=_params("parallel", "parallel", "arbitrary"),
        name="sb_attention",
    )(qkv3, qkv3, qkv3, tri)
    return out.reshape(batch * seq, D_MODEL)


def _out_proj_kernel(o_ref, w_ref, h_ref, out_ref):
    out_ref[...] = h_ref[...] + jnp.dot(o_ref[...], w_ref[...], preferred_element_type=F32)


def _out_proj(o, w, h, tm=1024):
    t, d = h.shape
    return pl.pallas_call(
        _out_proj_kernel,
        grid=(t // tm,),
        in_specs=[
            pl.BlockSpec((tm, o.shape[1]), lambda i: (i, 0)),
            pl.BlockSpec(w.shape, lambda i: (0, 0)),
            pl.BlockSpec((tm, d), lambda i: (i, 0)),
        ],
        out_specs=pl.BlockSpec((tm, d), lambda i: (i, 0)),
        out_shape=jax.ShapeDtypeStruct((t, d), F32),
        compiler_params=_params("parallel"),
        name="out_proj",
    )(o, w, h)


def _mlp_kernel(h_ref, g_ref, wu_ref, wd_ref, out_ref, xn_ref):
    f = pl.program_id(1)

    def contribution():
        u = jnp.maximum(jnp.dot(xn_ref[...], wu_ref[...], preferred_element_type=F32), 0.0)
        return jnp.dot((u * u).astype(BF16), wd_ref[...], preferred_element_type=F32)

    @pl.when(f == 0)
    def _():
        xn_ref[...] = _rms(h_ref[...], g_ref[...]).astype(BF16)
        out_ref[...] = h_ref[...] + contribution()

    @pl.when(f > 0)
    def _():
        out_ref[...] += contribution()


def _mlp(h, norm_g, w_up, w_down, tm=1024, fc=1024):
    t, d = h.shape
    ff = w_up.shape[1]
    return pl.pallas_call(
        _mlp_kernel,
        grid=(t // tm, ff // fc),
        in_specs=[
            pl.BlockSpec((tm, d), lambda i, f: (i, 0)),
            pl.BlockSpec((1, d), lambda i, f: (0, 0)),
            pl.BlockSpec((d, fc), lambda i, f: (0, f)),
            pl.BlockSpec((fc, d), lambda i, f: (f, 0)),
        ],
        out_specs=pl.BlockSpec((tm, d), lambda i, f: (i, 0)),
        out_shape=jax.ShapeDtypeStruct((t, d), F32),
        scratch_shapes=[pltpu.VMEM((tm, d), BF16)],
        compiler_params=_params("parallel", "arbitrary"),
        name="mlp",
    )(h, norm_g.reshape(1, d), w_up, w_down)


def _dil_attn_kernel(q_ref, kc_ref, kp_ref, vc_ref, vp_ref, o_ref, lse_ref, *, bias_per_step, max_steps):
    n = pl.program_id(2)
    blk = DIL_BLOCK
    qi = lax.broadcasted_iota(jnp.int32, (blk, 2 * blk), 0)
    kj = lax.broadcasted_iota(jnp.int32, (blk, 2 * blk), 1)
    steps = blk + qi - kj
    in_window = (steps >= 0) & (steps <= max_steps)
    steps_f = steps.astype(F32)
    lane = lax.broadcasted_iota(jnp.int32, (1, LANES), 1)
    lanes = [slice(head * DIL_HEAD_DIM, (head + 1) * DIL_HEAD_DIM) for head in range(DIL_HEADS)]

    def window(prev_ref, cur_ref, sub, sl):
        if sub == 0:
            return jnp.concatenate([prev_ref[:, sl], cur_ref[0:blk, sl]], axis=0)
        return cur_ref[(sub - 1) * blk:(sub + 1) * blk, sl]

    for sub in range(q_ref.shape[0] // blk):
        rows = slice(sub * blk, (sub + 1) * blk)
        valid = in_window & ((kj >= blk) | (n > 0)) if sub == 0 else in_window
        zs = [_dot_t(q_ref[rows, sl], window(kp_ref, kc_ref, sub, sl)) for sl in lanes]
        zs = [jnp.where(valid, z - bias_per_step[head] * steps_f, MASKED) for head, z in enumerate(zs)]
        ms = [jnp.max(z, axis=-1, keepdims=True) for z in zs]
        ps = [jnp.exp(z - m) for z, m in zip(zs, ms)]
        denoms = [jnp.sum(p, axis=-1, keepdims=True) for p in ps]
        pvs = [jnp.dot(p.astype(BF16), window(vp_ref, vc_ref, sub, sl), preferred_element_type=F32)
               for p, sl in zip(ps, lanes)]
        lse_tile = jnp.zeros((blk, LANES), F32)
        for head, sl in enumerate(lanes):
            o_ref[rows, sl] = pvs[head] / denoms[head]
            lse_tile = jnp.where(lane == head, ms[head] + jnp.log(denoms[head]), lse_tile)
        lse_ref[rows, :] = lse_tile


def _dil_attention(qkv, group, batch, seq):
    window, dilation = DIL_GROUPS[group]
    n_groups = len(DIL_GROUPS)
    blk = DIL_BLOCK
    length = seq // dilation
    assert length % blk == 0
    nb = length // blk
    width = DIL_HEADS * DIL_HEAD_DIM
    slopes = [2.0 ** (-8.0 * (group * DIL_HEADS + hd + 1) / (n_groups * DIL_HEADS)) for hd in range(DIL_HEADS)]
    view = qkv.reshape(batch, length, dilation * 3 * width)

    subs = DIL_BLOCKS_PER_STEP
    assert nb % subs == 0
    step_rows = subs * blk

    def col(offset):
        return lambda b, r, n: (b, n, r * 3 + offset)

    def col_prev(offset):
        return lambda b, r, n: (b, jnp.maximum(n * subs - 1, 0), r * 3 + offset)

    o, lse = pl.pallas_call(
        functools.partial(_dil_attn_kernel,
                          bias_per_step=tuple(s * dilation for s in slopes),
                          max_steps=window // dilation),
        grid=(batch, dilation, nb // subs),
        in_specs=[
            pl.BlockSpec((None, step_rows, width), col(0)),
            pl.BlockSpec((None, step_rows, width), col(1)),
            pl.BlockSpec((None, blk, width), col_prev(1)),
            pl.BlockSpec((None, step_rows, width), col(2)),
            pl.BlockSpec((None, blk, width), col_prev(2)),
        ],
        out_specs=[
            pl.BlockSpec((None, step_rows, width), lambda b, r, n: (b, n, r)),
            pl.BlockSpec((None, step_rows, LANES), lambda b, r, n: (b, n, r)),
        ],
        out_shape=[
            jax.ShapeDtypeStruct((batch, length, dilation * width), F32),
            jax.ShapeDtypeStruct((batch, length, dilation * LANES), F32),
        ],
        compiler_params=_params("parallel", "parallel", "arbitrary"),
        name="dil_attention_g%d" % group,
    )(view, view, view, view, view)
    rows = batch * length
    return o.reshape(rows, dilation * width), lse.reshape(rows, dilation * LANES)


def _dil_merge_kernel(o0_ref, o1_ref, o2_ref, l0_ref, l1_ref, l2_ref, w_ref, h_ref, out_ref,
                      o_tok_ref, l_tok_ref, mrg_ref):
    tm = h_ref.shape[0]
    slabs = D_MODEL // LANES
    o_in = (o0_ref, o1_ref, o2_ref)
    l_in = (l0_ref, l1_ref, l2_ref)
    for g, (_, dilation) in enumerate(DIL_GROUPS):
        for r in range(dilation):
            rows = slice(None) if dilation == 1 else pl.ds(r, tm // dilation, stride=dilation)
            l_tok_ref[g, rows, :] = l_in[g][:, r * LANES:(r + 1) * LANES]
            for c in range(slabs):
                lo = r * D_MODEL + c * LANES
                o_tok_ref[g * slabs + c, rows, :] = o_in[g][:, lo:lo + LANES]
    lses = [l_tok_ref[g] for g in range(len(DIL_GROUPS))]
    m = jnp.maximum(jnp.maximum(lses[0], lses[1]), lses[2])
    es = [jnp.exp(l - m) for l in lses]
    total = es[0] + es[1] + es[2]
    alphas = [e / total for e in es]
    for head in range(DIL_HEADS):
        merged = sum(alphas[g][:, head:head + 1] * o_tok_ref[g * slabs + head] for g in range(len(DIL_GROUPS)))
        mrg_ref[:, head * DIL_HEAD_DIM:(head + 1) * DIL_HEAD_DIM] = merged.astype(BF16)
    out_ref[...] = h_ref[...] + jnp.dot(mrg_ref[...], w_ref[...], preferred_element_type=F32)


def _dil_merge_out_proj(outs, lses, w, h, tm=512):
    t, d = h.shape
    n_groups = len(DIL_GROUPS)
    grouped = lambda cols, dil: pl.BlockSpec((tm // dil, dil * cols), lambda i: (i, 0))
    dils = [dil for _, dil in DIL_GROUPS]
    return pl.pallas_call(
        _dil_merge_kernel,
        grid=(t // tm,),
        in_specs=[grouped(d, dil) for dil in dils] + [grouped(LANES, dil) for dil in dils]
                 + [pl.BlockSpec(w.shape, lambda i: (0, 0)), pl.BlockSpec((tm, d), lambda i: (i, 0))],
        out_specs=pl.BlockSpec((tm, d), lambda i: (i, 0)),
        out_shape=jax.ShapeDtypeStruct((t, d), F32),
        scratch_shapes=[pltpu.VMEM((n_groups * d // LANES, tm, LANES), F32),
                        pltpu.VMEM((n_groups, tm, LANES), F32),
                        pltpu.VMEM((tm, d), BF16)],
        compiler_params=_params("parallel"),
        name="dil_merge_out_proj",
    )(*outs, *lses, w, h)


def _mla_proj_kernel(x_ref, g_ref, win_ref, qa_ref, kva_ref, wq_ref, wk_ref, wv_ref, qg_ref, kg_ref,
                     cos_ref, sin_up_ref, sin_dn_ref, ones_ref, q_out, k_out, v_out, qp_ref, kp_ref):
    xn = _rms(x_ref[...], g_ref[...]).astype(BF16)
    c = jnp.dot(xn, win_ref[...], preferred_element_type=F32)
    cq = _rms(c[:, :MLA_Q_RANK], qa_ref[...]).astype(BF16)
    ckv = _rms(c[:, MLA_Q_RANK:MLA_Q_RANK + MLA_KV_RANK], kva_ref[...]).astype(BF16)
    shared_rope = c[:, MLA_Q_RANK + MLA_KV_RANK:]
    qp_ref[...] = jnp.dot(cq, wq_ref[...], preferred_element_type=F32)
    kp_ref[...] = jnp.dot(ckv, wk_ref[...], preferred_element_type=F32)
    v = jnp.dot(ckv, wv_ref[...], preferred_element_type=F32)
    for s in range(v_out.shape[0]):
        v_out[s] = v[s * MLA_KEY_TILE:(s + 1) * MLA_KEY_TILE, :].T.astype(BF16)
    cos = cos_ref[...]
    sin_up = sin_up_ref[...]
    sin_dn = sin_dn_ref[...]
    half = MLA_ROPE // 2

    group = ones_ref.shape[0]

    def norm_rope(x, gain):
        y = x * lax.rsqrt(_split_dot(x * x, ones_ref[...]) + MLA_QK * EPS) * gain
        return y * cos + pltpu.roll(y, half, 1) * sin_up + pltpu.roll(y, group - half, 1) * sin_dn

    shared = jnp.concatenate([shared_rope] * (group // LANES), axis=1)
    for lo in range(0, MLA_HEADS * LANES, group):
        sl = slice(lo, lo + group)
        q_out[:, sl] = norm_rope(qp_ref[:, sl], qg_ref[...]).astype(BF16)
        k_out[:, sl] = norm_rope(kp_ref[:, sl] + shared, kg_ref[...]).astype(BF16)


def _pad_heads(w, heads, src_lo, src_hi, src_width):
    k = w.shape[0]
    w = w.reshape(k, heads, src_width)[:, :, src_lo:src_hi]
    w = jnp.pad(w, ((0, 0), (0, 0), (0, LANES - (src_hi - src_lo))))
    return w.reshape(k, heads * LANES)


def _mla_proj(h, norm_g, w_in, q_a_gain, kv_a_gain, w_q_b, w_kv_b, q_gain, k_gain, seq, tm=512):
    t, d = h.shape
    latent = MLA_Q_RANK + MLA_KV_RANK
    w_in_p = jnp.concatenate([
        w_in[:, :latent],
        jnp.zeros((d, MLA_NOPE), w_in.dtype),
        w_in[:, latent:],
        jnp.zeros((d, LANES - MLA_QK), w_in.dtype)], axis=1).astype(BF16)
    wq = _pad_heads(w_q_b, MLA_HEADS, 0, MLA_QK, MLA_QK).astype(BF16)
    wk = _pad_heads(w_kv_b, MLA_HEADS, 0, MLA_NOPE, MLA_NOPE + MLA_V).astype(BF16)
    wv = w_kv_b.reshape(MLA_KV_RANK, MLA_HEADS, MLA_NOPE + MLA_V)[:, :, MLA_NOPE:]
    wv = wv.reshape(MLA_KV_RANK, MLA_HEADS * MLA_V).astype(BF16)
    pad = jnp.zeros((LANES - MLA_QK,), F32)
    group = MXU_WIDTH // LANES
    qg = jnp.tile(jnp.concatenate([q_gain, pad]) * LOG2_E, group).reshape(1, MXU_WIDTH)
    kg = jnp.tile(jnp.concatenate([k_gain, pad]) * math.sqrt(MLA_QK), group).reshape(1, MXU_WIDTH)
    half = MLA_ROPE // 2
    inv = ROPE_THETA ** (-jnp.arange(half, dtype=F32) / half)
    ang = jnp.arange(seq, dtype=F32)[:, None] * inv[None, :]
    zeros = jnp.zeros((seq, half), F32)
    cos = jnp.concatenate([jnp.ones((seq, MLA_NOPE), F32), jnp.cos(ang), jnp.cos(ang),
                           jnp.zeros((seq, LANES - MLA_QK), F32)], axis=1)
    sin_up = jnp.concatenate([jnp.zeros((seq, MLA_NOPE), F32), zeros, jnp.sin(ang),
                              jnp.zeros((seq, LANES - MLA_QK), F32)], axis=1)
    sin_dn = jnp.concatenate([jnp.zeros((seq, MLA_NOPE), F32), -jnp.sin(ang), zeros,
                              jnp.zeros((seq, LANES - MLA_QK), F32)], axis=1)
    cos, sin_up, sin_dn = (jnp.tile(table, (1, group)) for table in (cos, sin_up, sin_dn))
    lane = jnp.arange(MXU_WIDTH)
    ones = (lane[:, None] // LANES == lane[None, :] // LANES).astype(BF16)
    per_seq = seq // tm
    full = lambda a: pl.BlockSpec(a.shape, lambda i: (0,) * a.ndim)
    table = pl.BlockSpec((tm, MXU_WIDTH), lambda i: (i % per_seq, 0))
    consts = (norm_g.reshape(1, d), w_in_p, q_a_gain.reshape(1, -1), kv_a_gain.reshape(1, -1), wq, wk, wv, qg, kg)
    qk_cols = MLA_HEADS * LANES
    v_rows = MLA_HEADS * MLA_V
    slabs = tm // MLA_KEY_TILE
    return pl.pallas_call(
        _mla_proj_kernel,
        grid=(t // tm,),
        in_specs=[pl.BlockSpec((tm, d), lambda i: (i, 0))] + [full(a) for a in consts]
                 + [table, table, table, full(ones)],
        out_specs=[
            pl.BlockSpec((tm, qk_cols), lambda i: (i, 0)),
            pl.BlockSpec((tm, qk_cols), lambda i: (i, 0)),
            pl.BlockSpec((slabs, v_rows, MLA_KEY_TILE), lambda i: (i, 0, 0)),
        ],
        out_shape=[
            jax.ShapeDtypeStruct((t, qk_cols), BF16),
            jax.ShapeDtypeStruct((t, qk_cols), BF16),
            jax.ShapeDtypeStruct((t // MLA_KEY_TILE, v_rows, MLA_KEY_TILE), BF16),
        ],
        scratch_shapes=[pltpu.VMEM((tm, qk_cols), F32), pltpu.VMEM((tm, qk_cols), F32)],
        compiler_params=_params("parallel"),
        name="mla_proj",
    )(h, *consts, cos, sin_up, sin_dn, ones)


def _mla_attn_kernel(q_ref, k_ref, vt_ref, o_ref):
    i = pl.program_id(2)
    heads = MLA_HEADS_PER_STEP
    sub = MLA_KEY_TILE
    bq = q_ref.shape[0]
    n_sub = bq // sub
    lanes = [slice(head * LANES, (head + 1) * LANES) for head in range(heads)]
    qs = [q_ref[:, sl] for sl in lanes]
    key_idx = lax.broadcasted_iota(jnp.int32, (bq, bq), 0)
    qry_idx = lax.broadcasted_iota(jnp.int32, (bq, bq), 1)

    def tile(j, state, diagonal):
        start = pl.multiple_of(j * bq, bq)
        keys = k_ref[pl.ds(start, bq), :]
        sts = [_dot_t(keys[:, lanes[head]], qs[head]) for head in range(heads)]
        if diagonal:
            sts = [jnp.where(key_idx <= qry_idx, st, MASKED) for st in sts]
        ms = [jnp.maximum(state[head][0], jnp.max(sts[head], axis=0, keepdims=True)) for head in range(heads)]
        pts = [jnp.exp2(sts[head] - ms[head]) for head in range(heads)]
        new_state = []
        for head in range(heads):
            m, l, acc = state[head]
            corr = jnp.exp2(m - ms[head])
            pv = sum(jnp.dot(vt_ref[j * n_sub + s, head * MLA_V:(head + 1) * MLA_V, :],
                             pts[head][s * sub:(s + 1) * sub].astype(BF16), preferred_element_type=F32)
                     for s in range(n_sub))
            new_state.append((ms[head], corr * l + jnp.sum(pts[head], axis=0, keepdims=True), corr * acc + pv))
        return tuple(new_state)

    def fixed_reference_tile(j, state):
        start = pl.multiple_of(j * bq, bq)
        keys = k_ref[pl.ds(start, bq), :]
        sts = [_dot_t(keys[:, lanes[head]], qs[head]) for head in range(heads)]
        pts = [jnp.exp2(sts[head] - state[head][0]) for head in range(heads)]
        new_state = []
        for head in range(heads):
            m, l, acc = state[head]
            pv = sum(jnp.dot(vt_ref[j * n_sub + s, head * MLA_V:(head + 1) * MLA_V, :],
                             pts[head][s * sub:(s + 1) * sub].astype(BF16), preferred_element_type=F32)
                     for s in range(n_sub))
            new_state.append((m, l + jnp.sum(pts[head], axis=0, keepdims=True), acc + pv))
        return tuple(new_state)

    init = tuple((jnp.full((1, bq), MASKED, F32), jnp.zeros((1, bq), F32), jnp.zeros((MLA_V, bq), F32))
                 for _ in range(heads))
    diag = tile(i, init, True)
    fast = lax.fori_loop(0, i, lambda t, s: fixed_reference_tile(t, s), diag)
    largest = functools.reduce(jnp.maximum, [l for _, l, _ in fast])
    in_range = jnp.max(largest) < MLA_MAX_DENOMINATOR

    def finish(state):
        out_t = jnp.concatenate([acc / l for _, l, acc in state], axis=0)
        return out_t.T.astype(BF16)

    o_ref[...] = lax.cond(in_range, lambda: finish(fast),
                          lambda: finish(lax.fori_loop(0, i, lambda t, s: tile(t, s, False), diag)))


def _mla_attention(q, k, vt, batch, seq, bq=512):
    heads = MLA_HEADS_PER_STEP
    sub = MLA_KEY_TILE
    q3 = q.reshape(batch, seq, -1)
    k3 = k.reshape(batch, seq, -1)
    vt4 = vt.reshape(batch, seq // sub, MLA_HEADS * MLA_V, sub)
    out = pl.pallas_call(
        _mla_attn_kernel,
        grid=(batch, MLA_HEADS // heads, seq // bq),
        in_specs=[
            pl.BlockSpec((None, bq, heads * LANES), lambda b, p, i: (b, i, p)),
            pl.BlockSpec((None, seq, heads * LANES), lambda b, p, i: (b, 0, p)),
            pl.BlockSpec((None, seq // sub, heads * MLA_V, sub), lambda b, p, i: (b, 0, p, 0)),
        ],
        out_specs=pl.BlockSpec((None, bq, heads * MLA_V), lambda b, p, i: (b, i, p)),
        out_shape=jax.ShapeDtypeStruct((batch, seq, MLA_HEADS * MLA_V), BF16),
        compiler_params=_params("parallel", "parallel", "arbitrary"),
        name="mla_attention",
    )(q3, k3, vt4)
    return out.reshape(batch * seq, MLA_HEADS * MLA_V)


def _sb_layer(h, batch, seq, mix_norm, w_in, q_norm, k_norm, w_out):
    scale = LOG2_E / math.sqrt(SB_HEAD_DIM)
    gains = jnp.stack([jnp.tile(q_norm, SB_HEADS) * scale, jnp.tile(k_norm, SB_HEADS)])
    qkv = _proj_qkv(h, mix_norm, w_in.astype(BF16), gains, SB_HEAD_DIM)
    o = _sb_attention(qkv, batch, seq)
    return _out_proj(o, w_out.astype(BF16), h)


def _dil_layer(h, batch, seq, mix_norm, w_in, q_norm, k_norm, w_out):
    scale = DIL_HEAD_DIM ** -0.5
    w_in = w_in.astype(BF16)
    cols = 3 * D_MODEL
    outs, lses = [], []
    for g, (_, dilation) in enumerate(DIL_GROUPS):
        gains = jnp.stack([jnp.tile(q_norm[g], DIL_HEADS) * scale, jnp.tile(k_norm[g], DIL_HEADS)])
        qkv = _proj_qkv(h, mix_norm, w_in[:, g * cols:(g + 1) * cols], gains, DIL_HEAD_DIM, dilation)
        o, lse = _dil_attention(qkv, g, batch, seq)
        outs.append(o)
        lses.append(lse)
    return _dil_merge_out_proj(outs, lses, w_out.astype(BF16), h)


def _mla_layer(h, batch, seq, mix_norm, w_in, q_a_norm, kv_a_norm, w_q_b, w_kv_b, q_norm, k_norm, w_out):
    q, k, v = _mla_proj(h, mix_norm, w_in, q_a_norm, kv_a_norm, w_q_b, w_kv_b, q_norm, k_norm, seq)
    o = _mla_attention(q, k, v, batch, seq)
    return _out_proj(o, w_out.astype(BF16), h)


def _trunk(x, layers):
    batch, seq, d = x.shape
    h = x.reshape(batch * seq, d)
    mixers = (_sb_layer, _dil_layer, _mla_layer)
    for idx, (mix_norm, mix_params, mlp_norm, w_up, w_down) in enumerate(layers):
        h = mixers[idx % len(mixers)](h, batch, seq, mix_norm, *mix_params)
        h = _mlp(h, mlp_norm, w_up.astype(BF16), w_down.astype(BF16))
    return h.reshape(batch, seq, d)


def kernel(x, l0_mix_norm, l0_sb_w_in, l0_sb_q_norm, l0_sb_k_norm, l0_sb_w_out, l0_mlp_norm, l0_mlp_w_up, l0_mlp_w_down, l1_mix_norm, l1_dil_w_in, l1_dil_q_norm, l1_dil_k_norm, l1_dil_w_out, l1_mlp_norm, l1_mlp_w_up, l1_mlp_w_down, l2_mix_norm, l2_mla_w_in, l2_mla_q_a_norm, l2_mla_kv_a_norm, l2_mla_w_q_b, l2_mla_w_kv_b, l2_mla_q_norm, l2_mla_k_norm, l2_mla_w_out, l2_mlp_norm, l2_mlp_w_up, l2_mlp_w_down, l3_mix_norm, l3_sb_w_in, l3_sb_q_norm, l3_sb_k_norm, l3_sb_w_out, l3_mlp_norm, l3_mlp_w_up, l3_mlp_w_down):
    layers = [
        (l0_mix_norm, (l0_sb_w_in, l0_sb_q_norm, l0_sb_k_norm, l0_sb_w_out),
         l0_mlp_norm, l0_mlp_w_up, l0_mlp_w_down),
        (l1_mix_norm, (l1_dil_w_in, l1_dil_q_norm, l1_dil_k_norm, l1_dil_w_out),
         l1_mlp_norm, l1_mlp_w_up, l1_mlp_w_down),
        (l2_mix_norm, (l2_mla_w_in, l2_mla_q_a_norm, l2_mla_kv_a_norm, l2_mla_w_q_b,
                       l2_mla_w_kv_b, l2_mla_q_norm, l2_mla_k_norm, l2_mla_w_out),
         l2_mlp_norm, l2_mlp_w_up, l2_mlp_w_down),
        (l3_mix_norm, (l3_sb_w_in, l3_sb_q_norm, l3_sb_k_norm, l3_sb_w_out),
         l3_mlp_norm, l3_mlp_w_up, l3_mlp_w_down),
    ]
    return _trunk(x, layers)
```

```python
import functools
import math

import jax
import jax.numpy as jnp
from jax import lax
from jax.experimental import pallas as pl
from jax.experimental.pallas import tpu as pltpu

F32 = jnp.float32
BF16 = jnp.bfloat16

EPS = 1e-6
LANES = 128
MXU_WIDTH = 256
MASKED = -1e30
LOG2_E = math.log2(math.e)
SB_SATURATED = 150.0
SB_LINEAR = 30.0
VMEM_LIMIT_BYTES = 48 * 1024 * 1024

D_MODEL = 1024
SB_HEADS = 16
SB_HEAD_DIM = 64
DIL_GROUPS = ((128, 1), (512, 4), (2048, 16))
DIL_HEADS = 8
DIL_HEAD_DIM = 128
DIL_BLOCK = 128
DIL_BLOCKS_PER_STEP = 2
MLA_HEADS = 16
MLA_NOPE = 64
MLA_ROPE = 32
MLA_V = 64
MLA_QK = MLA_NOPE + MLA_ROPE
MLA_Q_RANK = 384
MLA_KV_RANK = 256
MLA_KEY_TILE = 256
MLA_HEADS_PER_STEP = 4
MLA_MAX_DENOMINATOR = 2.0 ** 64
ROPE_THETA = 10000.0


def _params(*semantics):
    return pltpu.CompilerParams(dimension_semantics=semantics, vmem_limit_bytes=VMEM_LIMIT_BYTES)


def _rms(x, g):
    ms = jnp.mean(x * x, axis=-1, keepdims=True)
    return x * lax.rsqrt(ms + EPS) * g


def _split_dot(x, w):
    hi = x.astype(BF16)
    lo = (x - hi.astype(F32)).astype(BF16)
    return (jnp.dot(hi, w, preferred_element_type=F32)
            + jnp.dot(lo, w, preferred_element_type=F32))


def _dot_t(a, b):
    return lax.dot_general(a, b, (((1,), (1,)), ((), ())), preferred_element_type=F32)


def _proj_qkv_kernel(x_ref, g_ref, w_ref, gain_ref, seg_ref, o_ref, xn_ref, *y_scratch, head_dim, dilation):
    tm, width = xn_ref.shape[0], D_MODEL
    slabs = width // LANES
    seg_width = seg_ref.shape[0]
    xn_ref[...] = _rms(x_ref[...], g_ref[...]).astype(BF16)

    def project(block, lo):
        col = block * width + lo
        return jnp.dot(xn_ref[...], w_ref[:, col:col + seg_width], preferred_element_type=F32)

    def finish(block, lo, yc):
        if block < 2:
            sumsq = _split_dot(yc * yc, seg_ref[...])
            yc = yc * lax.rsqrt(sumsq + head_dim * EPS) * gain_ref[block:block + 1, lo:lo + seg_width]
        for c in range(seg_width // LANES):
            piece = yc[:, c * LANES:(c + 1) * LANES]
            if dilation == 1:
                o_ref[:, block * width + lo + c * LANES:block * width + lo + (c + 1) * LANES] = piece.astype(BF16)
            else:
                y_scratch[0][block * slabs + lo // LANES + c] = piece

    chunks = [(block, lo) for block in range(3) for lo in range(0, width, seg_width)]
    pending = project(*chunks[0])
    for previous, nxt in zip(chunks[:-1], chunks[1:]):
        following = project(*nxt)
        finish(*previous, pending)
        pending = following
    finish(*chunks[-1], pending)
    if dilation > 1:
        for block in range(3):
            for r in range(dilation):
                for c in range(slabs):
                    rows = y_scratch[0][block * slabs + c, pl.ds(r, tm // dilation, stride=dilation), :]
                    lo = (r * 3 + block) * width + c * LANES
                    o_ref[:, lo:lo + LANES] = rows.astype(BF16)


def _proj_qkv(h, norm_g, w, gains, head_dim, dilation=1, tm=512):
    t, d = h.shape
    width = D_MODEL
    assert w.shape == (d, 3 * width) and tm % (16 * dilation) == 0
    lane = jnp.arange(MXU_WIDTH)
    seg = (lane[:, None] // head_dim == lane[None, :] // head_dim).astype(BF16)
    gains = gains * math.sqrt(head_dim)
    return pl.pallas_call(
        functools.partial(_proj_qkv_kernel, head_dim=head_dim, dilation=dilation),
        grid=(t // tm,),
        in_specs=[
            pl.BlockSpec((tm, d), lambda i: (i, 0)),
            pl.BlockSpec((1, d), lambda i: (0, 0)),
            pl.BlockSpec((d, 3 * width), lambda i: (0, 0)),
            pl.BlockSpec((2, width), lambda i: (0, 0)),
            pl.BlockSpec((MXU_WIDTH, MXU_WIDTH), lambda i: (0, 0)),
        ],
        out_specs=pl.BlockSpec((tm // dilation, dilation * 3 * width), lambda i: (i, 0)),
        out_shape=jax.ShapeDtypeStruct((t // dilation, dilation * 3 * width), BF16),
        scratch_shapes=[pltpu.VMEM((tm, d), BF16)]
                       + ([pltpu.VMEM((3 * width // LANES, tm, LANES), F32)] if dilation > 1 else []),
        compiler_params=_params("parallel"),
        name="proj_qkv_d%d" % dilation,
    )(h, norm_g.reshape(1, d), w, gains, seg)


def _sb_attn_kernel(q_ref, k_ref, v_ref, tri_ref, o_ref, *, blk):
    i = pl.program_id(2)
    lane = lax.broadcasted_iota(jnp.int32, (1, LANES), 1)
    row = lax.broadcasted_iota(jnp.int32, (blk, blk), 0)
    col = lax.broadcasted_iota(jnp.int32, (blk, blk), 1)
    strictly_past = col < row
    per_group = LANES // SB_HEAD_DIM
    groups = q_ref.shape[1] // LANES
    heads = groups * per_group
    group_lanes = [slice((head // per_group) * LANES, (head // per_group + 1) * LANES) for head in range(heads)]
    head_lanes = [(lane // SB_HEAD_DIM) == head % per_group for head in range(heads)]
    qs = [jnp.where(head_lanes[head], q_ref[:, group_lanes[head]], jnp.zeros((blk, LANES), BF16))
          for head in range(heads)]

    def tiles(js, masks, state):
        starts = [pl.multiple_of(jnp.maximum(j, 0) * blk, blk) for j in js]
        pairs = [(t, head) for t in range(len(js)) for head in range(heads)]
        zs = {}
        for t, head in pairs:
            z = _dot_t(qs[head], k_ref[pl.ds(starts[t], blk), group_lanes[head]])
            zs[t, head] = z if masks[t] is None else jnp.where(masks[t], z, MASKED)
        sps = {p: jnp.where(zs[p] > SB_LINEAR, zs[p], jnp.log2(1.0 + jnp.exp2(zs[p]))) for p in pairs}
        withins = {p: _split_dot(sps[p], tri_ref[...]) for p in pairs}
        later = [state[head][1] for head in range(heads)]
        probs = {}
        for t, head in pairs:
            probs[t, head] = jnp.exp2(zs[t, head] - withins[t, head] - later[head]).astype(BF16)
            later[head] = later[head] + withins[t, head][:, 0:1]
        acc = [state[head][0] for head in range(heads)]
        for t, head in pairs:
            vals = v_ref[pl.ds(starts[t], blk), group_lanes[head]]
            acc[head] = acc[head] + jnp.dot(probs[t, head], vals, preferred_element_type=F32)
        return tuple((acc[head], later[head]) for head in range(heads))

    state = tuple((jnp.zeros((blk, LANES), F32), jnp.zeros((blk, 1), F32)) for _ in range(heads))
    state = tiles([i, i - 1], [strictly_past, i > 0], state)
    remaining = jnp.maximum(i - 1, 0)

    def unsaturated(s):
        lowest = functools.reduce(jnp.minimum, [later for _, later in s])
        return (jnp.min(lowest) < SB_SATURATED).astype(jnp.int32)

    def one_more(carry):
        t, _, s = carry
        s = tiles([i - 2 - t], [None], s)
        return t + 1, unsaturated(s), s

    _, _, state = lax.while_loop(lambda c: jnp.logical_and(c[0] < remaining, c[1] > 0), one_more,
                                 (jnp.int32(0), unsaturated(state), state))
    for group in range(groups):
        members = range(group * per_group, (group + 1) * per_group)
        out = sum(jnp.where(head_lanes[head], state[head][0], 0.0) for head in members)
        o_ref[:, group * LANES:(group + 1) * LANES] = out.astype(BF16)


def _sb_attention(qkv, batch, seq, blk=256, groups=2):
    width = groups * LANES
    steps = D_MODEL // width
    nq = seq // blk
    qkv3 = qkv.reshape(batch, seq, 3 * D_MODEL)
    idx = jnp.arange(blk)
    tri = (idx[:, None] >= idx[None, :]).astype(BF16)
    out = pl.pallas_call(
        functools.partial(_sb_attn_kernel, blk=blk),
        grid=(batch, steps, nq),
        in_specs=[
            pl.BlockSpec((None, blk, width), lambda b, p, i: (b, i, p)),
            pl.BlockSpec((None, seq, width), lambda b, p, i: (b, 0, steps + p)),
            pl.BlockSpec((None, seq, width), lambda b, p, i: (b, 0, 2 * steps + p)),
            pl.BlockSpec((blk, blk), lambda b, p, i: (0, 0)),
        ],
        out_specs=pl.BlockSpec((None, blk, width), lambda b, p, i: (b, i, p)),
        out_shape=jax.ShapeDtypeStruct((batch, seq, D_MODEL), BF16),
        compiler_params=_params("parallel", "parallel", "arbitrary"),
        name="sb_attention",
    )(qkv3, qkv3, qkv3, tri)
    return out.reshape(batch * seq, D_MODEL)


def _mlp_kernel(h_ref, o_ref, wo_ref, g_ref, wu_ref, wd_ref, out_ref, xn_ref):
    f = pl.program_id(1)

    def contribution():
        u = jnp.maximum(jnp.dot(xn_ref[...], wu_ref[...], preferred_element_type=F32), 0.0)
        return jnp.dot((u * u).astype(BF16), wd_ref[...], preferred_element_type=F32)

    @pl.when(f == 0)
    def _():
        h1 = h_ref[...] + jnp.dot(o_ref[...], wo_ref[...], preferred_element_type=F32)
        xn_ref[...] = _rms(h1, g_ref[...]).astype(BF16)
        out_ref[...] = h1 + contribution()

    @pl.when(f > 0)
    def _():
        out_ref[...] += contribution()


def _mlp(h, o, w_o, norm_g, w_up, w_down, tm=1024, fc=1024):
    t, d = h.shape
    ff = w_up.shape[1]
    return pl.pallas_call(
        _mlp_kernel,
        grid=(t // tm, ff // fc),
        in_specs=[
            pl.BlockSpec((tm, d), lambda i, f: (i, 0)),
            pl.BlockSpec((tm, o.shape[1]), lambda i, f: (i, 0)),
            pl.BlockSpec(w_o.shape, lambda i, f: (0, 0)),
            pl.BlockSpec((1, d), lambda i, f: (0, 0)),
            pl.BlockSpec((d, fc), lambda i, f: (0, f)),
            pl.BlockSpec((fc, d), lambda i, f: (f, 0)),
        ],
        out_specs=pl.BlockSpec((tm, d), lambda i, f: (i, 0)),
        out_shape=jax.ShapeDtypeStruct((t, d), F32),
        scratch_shapes=[pltpu.VMEM((tm, d), BF16)],
        compiler_params=_params("parallel", "arbitrary"),
        name="mlp",
    )(h, o, w_o, norm_g.reshape(1, d), w_up, w_down)


def _dil_attn_kernel(q_ref, kc_ref, kp_ref, vc_ref, vp_ref, o_ref, lse_ref, *, bias_per_step, max_steps):
    n = pl.program_id(2)
    blk = DIL_BLOCK
    qi = lax.broadcasted_iota(jnp.int32, (blk, 2 * blk), 0)
    kj = lax.broadcasted_iota(jnp.int32, (blk, 2 * blk), 1)
    steps = blk + qi - kj
    in_window = (steps >= 0) & (steps <= max_steps)
    steps_f = steps.astype(F32)
    lane = lax.broadcasted_iota(jnp.int32, (1, LANES), 1)
    lanes = [slice(head * DIL_HEAD_DIM, (head + 1) * DIL_HEAD_DIM) for head in range(DIL_HEADS)]

    def window(prev_ref, cur_ref, sub, sl):
        if sub == 0:
            return jnp.concatenate([prev_ref[:, sl], cur_ref[0:blk, sl]], axis=0)
        return cur_ref[(sub - 1) * blk:(sub + 1) * blk, sl]

    for sub in range(q_ref.shape[0] // blk):
        rows = slice(sub * blk, (sub + 1) * blk)
        valid = in_window & ((kj >= blk) | (n > 0)) if sub == 0 else in_window
        zs = [_dot_t(q_ref[rows, sl], window(kp_ref, kc_ref, sub, sl)) for sl in lanes]
        zs = [jnp.where(valid, z - bias_per_step[head] * steps_f, MASKED) for head, z in enumerate(zs)]
        ms = [jnp.max(z, axis=-1, keepdims=True) for z in zs]
        ps = [jnp.exp(z - m) for z, m in zip(zs, ms)]
        denoms = [jnp.sum(p, axis=-1, keepdims=True) for p in ps]
        pvs = [jnp.dot(p.astype(BF16), window(vp_ref, vc_ref, sub, sl), preferred_element_type=F32)
               for p, sl in zip(ps, lanes)]
        lse_tile = jnp.zeros((blk, LANES), F32)
        for head, sl in enumerate(lanes):
            o_ref[rows, sl] = pvs[head] / denoms[head]
            lse_tile = jnp.where(lane == head, ms[head] + jnp.log(denoms[head]), lse_tile)
        lse_ref[rows, :] = lse_tile


def _dil_attention(qkv, group, batch, seq):
    window, dilation = DIL_GROUPS[group]
    n_groups = len(DIL_GROUPS)
    blk = DIL_BLOCK
    length = seq // dilation
    assert length % blk == 0
    nb = length // blk
    width = DIL_HEADS * DIL_HEAD_DIM
    slopes = [2.0 ** (-8.0 * (group * DIL_HEADS + hd + 1) / (n_groups * DIL_HEADS)) for hd in range(DIL_HEADS)]
    view = qkv.reshape(batch, length, dilation * 3 * width)

    subs = DIL_BLOCKS_PER_STEP
    assert nb % subs == 0
    step_rows = subs * blk

    def col(offset):
        return lambda b, r, n: (b, n, r * 3 + offset)

    def col_prev(offset):
        return lambda b, r, n: (b, jnp.maximum(n * subs - 1, 0), r * 3 + offset)

    o, lse = pl.pallas_call(
        functools.partial(_dil_attn_kernel,
                          bias_per_step=tuple(s * dilation for s in slopes),
                          max_steps=window // dilation),
        grid=(batch, dilation, nb // subs),
        in_specs=[
            pl.BlockSpec((None, step_rows, width), col(0)),
            pl.BlockSpec((None, step_rows, width), col(1)),
            pl.BlockSpec((None, blk, width), col_prev(1)),
            pl.BlockSpec((None, step_rows, width), col(2)),
            pl.BlockSpec((None, blk, width), col_prev(2)),
        ],
        out_specs=[
            pl.BlockSpec((None, step_rows, width), lambda b, r, n: (b, n, r)),
            pl.BlockSpec((None, step_rows, LANES), lambda b, r, n: (b, n, r)),
        ],
        out_shape=[
            jax.ShapeDtypeStruct((batch, length, dilation * width), F32),
            jax.ShapeDtypeStruct((batch, length, dilation * LANES), F32),
        ],
        compiler_params=_params("parallel", "parallel", "arbitrary"),
        name="dil_attention_g%d" % group,
    )(view, view, view, view, view)
    rows = batch * length
    return o.reshape(rows, dilation * width), lse.reshape(rows, dilation * LANES)


def _dil_merge_kernel(o0_ref, o1_ref, o2_ref, l0_ref, l1_ref, l2_ref, mrg_ref, o_tok_ref, l_tok_ref):
    tm = mrg_ref.shape[0]
    slabs = D_MODEL // LANES
    o_in = (o0_ref, o1_ref, o2_ref)
    l_in = (l0_ref, l1_ref, l2_ref)
    for g, (_, dilation) in enumerate(DIL_GROUPS):
        for r in range(dilation):
            rows = slice(None) if dilation == 1 else pl.ds(r, tm // dilation, stride=dilation)
            l_tok_ref[g, rows, :] = l_in[g][:, r * LANES:(r + 1) * LANES]
            for c in range(slabs):
                lo = r * D_MODEL + c * LANES
                o_tok_ref[g * slabs + c, rows, :] = o_in[g][:, lo:lo + LANES]
    lses = [l_tok_ref[g] for g in range(len(DIL_GROUPS))]
    m = jnp.maximum(jnp.maximum(lses[0], lses[1]), lses[2])
    es = [jnp.exp(l - m) for l in lses]
    total = es[0] + es[1] + es[2]
    alphas = [e / total for e in es]
    for head in range(DIL_HEADS):
        merged = sum(alphas[g][:, head:head + 1] * o_tok_ref[g * slabs + head] for g in range(len(DIL_GROUPS)))
        mrg_ref[:, head * DIL_HEAD_DIM:(head + 1) * DIL_HEAD_DIM] = merged.astype(BF16)


def _dil_merge(outs, lses, t, tm=512):
    d = D_MODEL
    n_groups = len(DIL_GROUPS)
    grouped = lambda cols, dil: pl.BlockSpec((tm // dil, dil * cols), lambda i: (i, 0))
    dils = [dil for _, dil in DIL_GROUPS]
    return pl.pallas_call(
        _dil_merge_kernel,
        grid=(t // tm,),
        in_specs=[grouped(d, dil) for dil in dils] + [grouped(LANES, dil) for dil in dils],
        out_specs=pl.BlockSpec((tm, d), lambda i: (i, 0)),
        out_shape=jax.ShapeDtypeStruct((t, d), BF16),
        scratch_shapes=[pltpu.VMEM((n_groups * d // LANES, tm, LANES), F32),
                        pltpu.VMEM((n_groups, tm, LANES), F32)],
        compiler_params=_params("parallel"),
        name="dil_merge",
    )(*outs, *lses)


def _mla_proj_kernel(x_ref, g_ref, win_ref, qa_ref, kva_ref, wq_ref, wk_ref, wv_ref, qg_ref, kg_ref,
                     cos_ref, sin_up_ref, sin_dn_ref, ones_ref, q_out, k_out, v_out, qp_ref, kp_ref):
    xn = _rms(x_ref[...], g_ref[...]).astype(BF16)
    c = jnp.dot(xn, win_ref[...], preferred_element_type=F32)
    cq = _rms(c[:, :MLA_Q_RANK], qa_ref[...]).astype(BF16)
    ckv = _rms(c[:, MLA_Q_RANK:MLA_Q_RANK + MLA_KV_RANK], kva_ref[...]).astype(BF16)
    shared_rope = c[:, MLA_Q_RANK + MLA_KV_RANK:]
    qp_ref[...] = jnp.dot(cq, wq_ref[...], preferred_element_type=F32)
    kp_ref[...] = jnp.dot(ckv, wk_ref[...], preferred_element_type=F32)
    v = jnp.dot(ckv, wv_ref[...], preferred_element_type=F32)
    for s in range(v_out.shape[0]):
        v_out[s] = v[s * MLA_KEY_TILE:(s + 1) * MLA_KEY_TILE, :].T.astype(BF16)
    cos = cos_ref[...]
    sin_up = sin_up_ref[...]
    sin_dn = sin_dn_ref[...]
    half = MLA_ROPE // 2

    group = ones_ref.shape[0]

    def norm_rope(x, gain):
        y = x * lax.rsqrt(_split_dot(x * x, ones_ref[...]) + MLA_QK * EPS) * gain
        return y * cos + pltpu.roll(y, half, 1) * sin_up + pltpu.roll(y, group - half, 1) * sin_dn

    shared = jnp.concatenate([shared_rope] * (group // LANES), axis=1)
    for lo in range(0, MLA_HEADS * LANES, group):
        sl = slice(lo, lo + group)
        q_out[:, sl] = norm_rope(qp_ref[:, sl], qg_ref[...]).astype(BF16)
        k_out[:, sl] = norm_rope(kp_ref[:, sl] + shared, kg_ref[...]).astype(BF16)


def _pad_heads(w, heads, src_lo, src_hi, src_width):
    k = w.shape[0]
    w = w.reshape(k, heads, src_width)[:, :, src_lo:src_hi]
    w = jnp.pad(w, ((0, 0), (0, 0), (0, LANES - (src_hi - src_lo))))
    return w.reshape(k, heads * LANES)


def _mla_proj(h, norm_g, w_in, q_a_gain, kv_a_gain, w_q_b, w_kv_b, q_gain, k_gain, seq, tm=512):
    t, d = h.shape
    latent = MLA_Q_RANK + MLA_KV_RANK
    w_in_p = jnp.concatenate([
        w_in[:, :latent],
        jnp.zeros((d, MLA_NOPE), w_in.dtype),
        w_in[:, latent:],
        jnp.zeros((d, LANES - MLA_QK), w_in.dtype)], axis=1).astype(BF16)
    wq = _pad_heads(w_q_b, MLA_HEADS, 0, MLA_QK, MLA_QK).astype(BF16)
    wk = _pad_heads(w_kv_b, MLA_HEADS, 0, MLA_NOPE, MLA_NOPE + MLA_V).astype(BF16)
    wv = w_kv_b.reshape(MLA_KV_RANK, MLA_HEADS, MLA_NOPE + MLA_V)[:, :, MLA_NOPE:]
    wv = wv.reshape(MLA_KV_RANK, MLA_HEADS * MLA_V).astype(BF16)
    pad = jnp.zeros((LANES - MLA_QK,), F32)
    group = MXU_WIDTH // LANES
    qg = jnp.tile(jnp.concatenate([q_gain, pad]) * LOG2_E, group).reshape(1, MXU_WIDTH)
    kg = jnp.tile(jnp.concatenate([k_gain, pad]) * math.sqrt(MLA_QK), group).reshape(1, MXU_WIDTH)
    half = MLA_ROPE // 2
    inv = ROPE_THETA ** (-jnp.arange(half, dtype=F32) / half)
    ang = jnp.arange(seq, dtype=F32)[:, None] * inv[None, :]
    zeros = jnp.zeros((seq, half), F32)
    cos = jnp.concatenate([jnp.ones((seq, MLA_NOPE), F32), jnp.cos(ang), jnp.cos(ang),
                           jnp.zeros((seq, LANES - MLA_QK), F32)], axis=1)
    sin_up = jnp.concatenate([jnp.zeros((seq, MLA_NOPE), F32), zeros, jnp.sin(ang),
                              jnp.zeros((seq, LANES - MLA_QK), F32)], axis=1)
    sin_dn = jnp.concatenate([jnp.zeros((seq, MLA_NOPE), F32), -jnp.sin(ang), zeros,
                              jnp.zeros((seq, LANES - MLA_QK), F32)], axis=1)
    cos, sin_up, sin_dn = (jnp.tile(table, (1, group)) for table in (cos, sin_up, sin_dn))
    lane = jnp.arange(MXU_WIDTH)
    ones = (lane[:, None] // LANES == lane[None, :] // LANES).astype(BF16)
    per_seq = seq // tm
    full = lambda a: pl.BlockSpec(a.shape, lambda i: (0,) * a.ndim)
    table = pl.BlockSpec((tm, MXU_WIDTH), lambda i: (i % per_seq, 0))
    consts = (norm_g.reshape(1, d), w_in_p, q_a_gain.reshape(1, -1), kv_a_gain.reshape(1, -1), wq, wk, wv, qg, kg)
    qk_cols = MLA_HEADS * LANES
    v_rows = MLA_HEADS * MLA_V
    slabs = tm // MLA_KEY_TILE
    return pl.pallas_call(
        _mla_proj_kernel,
        grid=(t // tm,),
        in_specs=[pl.BlockSpec((tm, d), lambda i: (i, 0))] + [full(a) for a in consts]
                 + [table, table, table, full(ones)],
        out_specs=[
            pl.BlockSpec((tm, qk_cols), lambda i: (i, 0)),
            pl.BlockSpec((tm, qk_cols), lambda i: (i, 0)),
            pl.BlockSpec((slabs, v_rows, MLA_KEY_TILE), lambda i: (i, 0, 0)),
        ],
        out_shape=[
            jax.ShapeDtypeStruct((t, qk_cols), BF16),
            jax.ShapeDtypeStruct((t, qk_cols), BF16),
            jax.ShapeDtypeStruct((t // MLA_KEY_TILE, v_rows, MLA_KEY_TILE), BF16),
        ],
        scratch_shapes=[pltpu.VMEM((tm, qk_cols), F32), pltpu.VMEM((tm, qk_cols), F32)],
        compiler_params=_params("parallel"),
        name="mla_proj",
    )(h, *consts, cos, sin_up, sin_dn, ones)


def _mla_attn_kernel(q_ref, k_ref, vt_ref, o_ref):
    i = pl.program_id(2)
    heads = MLA_HEADS_PER_STEP
    sub = MLA_KEY_TILE
    bq = q_ref.shape[0]
    n_sub = bq // sub
    lanes = [slice(head * LANES, (head + 1) * LANES) for head in range(heads)]
    qs = [q_ref[:, sl] for sl in lanes]
    key_idx = lax.broadcasted_iota(jnp.int32, (bq, bq), 0)
    qry_idx = lax.broadcasted_iota(jnp.int32, (bq, bq), 1)

    def tile(j, state, diagonal):
        start = pl.multiple_of(j * bq, bq)
        keys = k_ref[pl.ds(start, bq), :]
        sts = [_dot_t(keys[:, lanes[head]], qs[head]) for head in range(heads)]
        if diagonal:
            sts = [jnp.where(key_idx <= qry_idx, st, MASKED) for st in sts]
        ms = [jnp.maximum(state[head][0], jnp.max(sts[head], axis=0, keepdims=True)) for head in range(heads)]
        pts = [jnp.exp2(sts[head] - ms[head]) for head in range(heads)]
        new_state = []
        for head in range(heads):
            m, l, acc = state[head]
            corr = jnp.exp2(m - ms[head])
            pv = sum(jnp.dot(vt_ref[j * n_sub + s, head * MLA_V:(head + 1) * MLA_V, :],
                             pts[head][s * sub:(s + 1) * sub].astype(BF16), preferred_element_type=F32)
                     for s in range(n_sub))
            new_state.append((ms[head], corr * l + jnp.sum(pts[head], axis=0, keepdims=True), corr * acc + pv))
        return tuple(new_state)

    def fixed_reference_tile(j, state):
        start = pl.multiple_of(j * bq, bq)
        keys = k_ref[pl.ds(start, bq), :]
        sts = [_dot_t(keys[:, lanes[head]], qs[head]) for head in range(heads)]
        pts = [jnp.exp2(sts[head] - state[head][0]) for head in range(heads)]
        new_state = []
        for head in range(heads):
            m, l, acc = state[head]
            pv = sum(jnp.dot(vt_ref[j * n_sub + s, head * MLA_V:(head + 1) * MLA_V, :],
                             pts[head][s * sub:(s + 1) * sub].astype(BF16), preferred_element_type=F32)
                     for s in range(n_sub))
            new_state.append((m, l + jnp.sum(pts[head], axis=0, keepdims=True), acc + pv))
        return tuple(new_state)

    init = tuple((jnp.full((1, bq), MASKED, F32), jnp.zeros((1, bq), F32), jnp.zeros((MLA_V, bq), F32))
                 for _ in range(heads))
    diag = tile(i, init, True)
    fast = lax.fori_loop(0, i, lambda t, s: fixed_reference_tile(t, s), diag)
    largest = functools.reduce(jnp.maximum, [l for _, l, _ in fast])
    in_range = jnp.max(largest) < MLA_MAX_DENOMINATOR

    def finish(state):
        out_t = jnp.concatenate([acc / l for _, l, acc in state], axis=0)
        return out_t.T.astype(BF16)

    o_ref[...] = lax.cond(in_range, lambda: finish(fast),
                          lambda: finish(lax.fori_loop(0, i, lambda t, s: tile(t, s, False), diag)))


def _mla_attention(q, k, vt, batch, seq, bq=512):
    heads = MLA_HEADS_PER_STEP
    sub = MLA_KEY_TILE
    q3 = q.reshape(batch, seq, -1)
    k3 = k.reshape(batch, seq, -1)
    vt4 = vt.reshape(batch, seq // sub, MLA_HEADS * MLA_V, sub)
    out = pl.pallas_call(
        _mla_attn_kernel,
        grid=(batch, MLA_HEADS // heads, seq // bq),
        in_specs=[
            pl.BlockSpec((None, bq, heads * LANES), lambda b, p, i: (b, i, p)),
            pl.BlockSpec((None, seq, heads * LANES), lambda b, p, i: (b, 0, p)),
            pl.BlockSpec((None, seq // sub, heads * MLA_V, sub), lambda b, p, i: (b, 0, p, 0)),
        ],
        out_specs=pl.BlockSpec((None, bq, heads * MLA_V), lambda b, p, i: (b, i, p)),
        out_shape=jax.ShapeDtypeStruct((batch, seq, MLA_HEADS * MLA_V), BF16),
        compiler_params=_params("parallel", "parallel", "arbitrary"),
        name="mla_attention",
    )(q3, k3, vt4)
    return out.reshape(batch * seq, MLA_HEADS * MLA_V)


def _sb_layer(h, batch, seq, mix_norm, w_in, q_norm, k_norm, w_out):
    scale = LOG2_E / math.sqrt(SB_HEAD_DIM)
    gains = jnp.stack([jnp.tile(q_norm, SB_HEADS) * scale, jnp.tile(k_norm, SB_HEADS)])
    qkv = _proj_qkv(h, mix_norm, w_in.astype(BF16), gains, SB_HEAD_DIM)
    return _sb_attention(qkv, batch, seq), w_out


def _dil_layer(h, batch, seq, mix_norm, w_in, q_norm, k_norm, w_out):
    scale = DIL_HEAD_DIM ** -0.5
    w_in = w_in.astype(BF16)
    cols = 3 * D_MODEL
    outs, lses = [], []
    for g, (_, dilation) in enumerate(DIL_GROUPS):
        gains = jnp.stack([jnp.tile(q_norm[g], DIL_HEADS) * scale, jnp.tile(k_norm[g], DIL_HEADS)])
        qkv = _proj_qkv(h, mix_norm, w_in[:, g * cols:(g + 1) * cols], gains, DIL_HEAD_DIM, dilation)
        o, lse = _dil_attention(qkv, g, batch, seq)
        outs.append(o)
        lses.append(lse)
    return _dil_merge(outs, lses, h.shape[0]), w_out


def _mla_layer(h, batch, seq, mix_norm, w_in, q_a_norm, kv_a_norm, w_q_b, w_kv_b, q_norm, k_norm, w_out):
    q, k, v = _mla_proj(h, mix_norm, w_in, q_a_norm, kv_a_norm, w_q_b, w_kv_b, q_norm, k_norm, seq)
    return _mla_attention(q, k, v, batch, seq), w_out


def _trunk(x, layers):
    batch, seq, d = x.shape
    h = x.reshape(batch * seq, d)
    mixers = (_sb_layer, _dil_layer, _mla_layer)
    for idx, (mix_norm, mix_params, mlp_norm, w_up, w_down) in enumerate(layers):
        o, w_out = mixers[idx % len(mixers)](h, batch, seq, mix_norm, *mix_params)
        h = _mlp(h, o, w_out.astype(BF16), mlp_norm, w_up.astype(BF16), w_down.astype(BF16))
    return h.reshape(batch, seq, d)


def kernel(x, l0_mix_norm, l0_sb_w_in, l0_sb_q_norm, l0_sb_k_norm, l0_sb_w_out, l0_mlp_norm, l0_mlp_w_up, l0_mlp_w_down, l1_mix_norm, l1_dil_w_in, l1_dil_q_norm, l1_dil_k_norm, l1_dil_w_out, l1_mlp_norm, l1_mlp_w_up, l1_mlp_w_down, l2_mix_norm, l2_mla_w_in, l2_mla_q_a_norm, l2_mla_kv_a_norm, l2_mla_w_q_b, l2_mla_w_kv_b, l2_mla_q_norm, l2_mla_k_norm, l2_mla_w_out, l2_mlp_norm, l2_mlp_w_up, l2_mlp_w_down, l3_mix_norm, l3_sb_w_in, l3_sb_q_norm, l3_sb_k_norm, l3_sb_w_out, l3_mlp_norm, l3_mlp_w_up, l3_mlp_w_down):
    layers = [
        (l0_mix_norm, (l0_sb_w_in, l0_sb_q_norm, l0_sb_k_norm, l0_sb_w_out),
         l0_mlp_norm, l0_mlp_w_up, l0_mlp_w_down),
        (l1_mix_norm, (l1_dil_w_in, l1_dil_q_norm, l1_dil_k_norm, l1_dil_w_out),
         l1_mlp_norm, l1_mlp_w_up, l1_mlp_w_down),
        (l2_mix_norm, (l2_mla_w_in, l2_mla_q_a_norm, l2_mla_kv_a_norm, l2_mla_w_q_b,
                       l2_mla_w_kv_b, l2_mla_q_norm, l2_mla_k_norm, l2_mla_w_out),
         l2_mlp_norm, l2_mlp_w_up, l2_mlp_w_down),
        (l3_mix_norm, (l3_sb_w_in, l3_sb_q_norm, l3_sb_k_norm, l3_sb_w_out),
         l3_mlp_norm, l3_mlp_w_up, l3_mlp_w_down),
    ]
    return _trunk(x, layers)
```

```python
import functools
import math

import jax
import jax.numpy as jnp
from jax import lax
from jax.experimental import pallas as pl
from jax.experimental.pallas import tpu as pltpu

F32 = jnp.float32
BF16 = jnp.bfloat16

EPS = 1e-6
LANES = 128
MXU_WIDTH = 256
MASKED = -1e30
LOG2_E = math.log2(math.e)
SB_SATURATED = 150.0
SB_LINEAR = 30.0
VMEM_LIMIT_BYTES = 48 * 1024 * 1024

D_MODEL = 1024
SB_HEADS = 16
SB_HEAD_DIM = 64
DIL_GROUPS = ((128, 1), (512, 4), (2048, 16))
DIL_HEADS = 8
DIL_HEAD_DIM = 128
DIL_BLOCK = 128
DIL_BLOCKS_PER_STEP = 2
MLA_HEADS = 16
MLA_NOPE = 64
MLA_ROPE = 32
MLA_V = 64
MLA_QK = MLA_NOPE + MLA_ROPE
MLA_Q_RANK = 384
MLA_KV_RANK = 256
MLA_KEY_TILE = 256
MLA_HEADS_PER_STEP = 4
MLA_MAX_DENOMINATOR = 2.0 ** 64
ROPE_THETA = 10000.0


def _params(*semantics):
    return pltpu.CompilerParams(dimension_semantics=semantics, vmem_limit_bytes=VMEM_LIMIT_BYTES)


def _rms(x, g):
    ms = jnp.mean(x * x, axis=-1, keepdims=True)
    return x * lax.rsqrt(ms + EPS) * g


def _split_dot(x, w):
    hi = x.astype(BF16)
    lo = (x - hi.astype(F32)).astype(BF16)
    return (jnp.dot(hi, w, preferred_element_type=F32)
            + jnp.dot(lo, w, preferred_element_type=F32))


def _dot_t(a, b):
    return lax.dot_general(a, b, (((1,), (1,)), ((), ())), preferred_element_type=F32)


def _proj_qkv_kernel(x_ref, g_ref, w_ref, gain_ref, seg_ref, o_ref, xn_ref, *y_scratch, head_dim, dilation):
    tm, width = xn_ref.shape[0], D_MODEL
    slabs = width // LANES
    seg_width = seg_ref.shape[0]
    xn_ref[...] = _rms(x_ref[...], g_ref[...]).astype(BF16)

    def project(block, lo):
        col = block * width + lo
        return jnp.dot(xn_ref[...], w_ref[:, col:col + seg_width], preferred_element_type=F32)

    def finish(block, lo, yc):
        if block < 2:
            sumsq = _split_dot(yc * yc, seg_ref[...])
            yc = yc * lax.rsqrt(sumsq + head_dim * EPS) * gain_ref[block:block + 1, lo:lo + seg_width]
        for c in range(seg_width // LANES):
            piece = yc[:, c * LANES:(c + 1) * LANES]
            if dilation == 1:
                o_ref[:, block * width + lo + c * LANES:block * width + lo + (c + 1) * LANES] = piece.astype(BF16)
            else:
                y_scratch[0][block * slabs + lo // LANES + c] = piece

    chunks = [(block, lo) for block in range(3) for lo in range(0, width, seg_width)]
    pending = project(*chunks[0])
    for previous, nxt in zip(chunks[:-1], chunks[1:]):
        following = project(*nxt)
        finish(*previous, pending)
        pending = following
    finish(*chunks[-1], pending)
    if dilation > 1:
        for block in range(3):
            for r in range(dilation):
                for c in range(slabs):
                    rows = y_scratch[0][block * slabs + c, pl.ds(r, tm // dilation, stride=dilation), :]
                    lo = (r * 3 + block) * width + c * LANES
                    o_ref[:, lo:lo + LANES] = rows.astype(BF16)


def _proj_qkv(h, norm_g, w, gains, head_dim, dilation=1, tm=512):
    t, d = h.shape
    width = D_MODEL
    assert w.shape == (d, 3 * width) and tm % (16 * dilation) == 0
    lane = jnp.arange(MXU_WIDTH)
    seg = (lane[:, None] // head_dim == lane[None, :] // head_dim).astype(BF16)
    gains = gains * math.sqrt(head_dim)
    return pl.pallas_call(
        functools.partial(_proj_qkv_kernel, head_dim=head_dim, dilation=dilation),
        grid=(t // tm,),
        in_specs=[
            pl.BlockSpec((tm, d), lambda i: (i, 0)),
            pl.BlockSpec((1, d), lambda i: (0, 0)),
            pl.BlockSpec((d, 3 * width), lambda i: (0, 0)),
            pl.BlockSpec((2, width), lambda i: (0, 0)),
            pl.BlockSpec((MXU_WIDTH, MXU_WIDTH), lambda i: (0, 0)),
        ],
        out_specs=pl.BlockSpec((tm // dilation, dilation * 3 * width), lambda i: (i, 0)),
        out_shape=jax.ShapeDtypeStruct((t // dilation, dilation * 3 * width), BF16),
        scratch_shapes=[pltpu.VMEM((tm, d), BF16)]
                       + ([pltpu.VMEM((3 * width // LANES, tm, LANES), F32)] if dilation > 1 else []),
        compiler_params=_params("parallel"),
        name="proj_qkv_d%d" % dilation,
    )(h, norm_g.reshape(1, d), w, gains, seg)


def _sb_attn_kernel(q_ref, k_ref, v_ref, tri_ref, o_ref, *, blk):
    i = pl.program_id(2)
    lane = lax.broadcasted_iota(jnp.int32, (1, LANES), 1)
    row = lax.broadcasted_iota(jnp.int32, (blk, blk), 0)
    col = lax.broadcasted_iota(jnp.int32, (blk, blk), 1)
    strictly_past = col < row
    per_group = LANES // SB_HEAD_DIM
    groups = q_ref.shape[1] // LANES
    heads = groups * per_group
    group_lanes = [slice((head // per_group) * LANES, (head // per_group + 1) * LANES) for head in range(heads)]
    head_lanes = [(lane // SB_HEAD_DIM) == head % per_group for head in range(heads)]
    qs = [jnp.where(head_lanes[head], q_ref[:, group_lanes[head]], jnp.zeros((blk, LANES), BF16))
          for head in range(heads)]

    def tiles(js, masks, state):
        starts = [pl.multiple_of(jnp.maximum(j, 0) * blk, blk) for j in js]
        pairs = [(t, head) for t in range(len(js)) for head in range(heads)]
        zs = {}
        for t, head in pairs:
            z = _dot_t(qs[head], k_ref[pl.ds(starts[t], blk), group_lanes[head]])
            zs[t, head] = z if masks[t] is None else jnp.where(masks[t], z, MASKED)
        sps = {p: jnp.where(zs[p] > SB_LINEAR, zs[p], jnp.log2(1.0 + jnp.exp2(zs[p]))) for p in pairs}
        withins = {p: _split_dot(sps[p], tri_ref[...]) for p in pairs}
        later = [state[head][1] for head in range(heads)]
        probs = {}
        for t, head in pairs:
            probs[t, head] = jnp.exp2(zs[t, head] - withins[t, head] - later[head]).astype(BF16)
            later[head] = later[head] + withins[t, head][:, 0:1]
        acc = [state[head][0] for head in range(heads)]
        for t, head in pairs:
            vals = v_ref[pl.ds(starts[t], blk), group_lanes[head]]
            acc[head] = acc[head] + jnp.dot(probs[t, head], vals, preferred_element_type=F32)
        return tuple((acc[head], later[head]) for head in range(heads))

    state = tuple((jnp.zeros((blk, LANES), F32), jnp.zeros((blk, 1), F32)) for _ in range(heads))
    state = tiles([i, i - 1], [strictly_past, i > 0], state)
    remaining = jnp.maximum(i - 1, 0)

    def unsaturated(s):
        lowest = functools.reduce(jnp.minimum, [later for _, later in s])
        return (jnp.min(lowest) < SB_SATURATED).astype(jnp.int32)

    def one_more(carry):
        t, _, s = carry
        s = tiles([i - 2 - t], [None], s)
        return t + 1, unsaturated(s), s

    _, _, state = lax.while_loop(lambda c: jnp.logical_and(c[0] < remaining, c[1] > 0), one_more,
                                 (jnp.int32(0), unsaturated(state), state))
    for group in range(groups):
        members = range(group * per_group, (group + 1) * per_group)
        out = sum(jnp.where(head_lanes[head], state[head][0], 0.0) for head in members)
        o_ref[:, group * LANES:(group + 1) * LANES] = out.astype(BF16)


def _sb_attention(qkv, batch, seq, blk=256, groups=2):
    width = groups * LANES
    steps = D_MODEL // width
    nq = seq // blk
    qkv3 = qkv.reshape(batch, seq, 3 * D_MODEL)
    idx = jnp.arange(blk)
    tri = (idx[:, None] >= idx[None, :]).astype(BF16)
    out = pl.pallas_call(
        functools.partial(_sb_attn_kernel, blk=blk),
        grid=(batch, steps, nq),
        in_specs=[
            pl.BlockSpec((None, blk, width), lambda b, p, i: (b, i, p)),
            pl.BlockSpec((None, seq, width), lambda b, p, i: (b, 0, steps + p)),
            pl.BlockSpec((None, seq, width), lambda b, p, i: (b, 0, 2 * steps + p)),
            pl.BlockSpec((blk, blk), lambda b, p, i: (0, 0)),
        ],
        out_specs=pl.BlockSpec((None, blk, width), lambda b, p, i: (b, i, p)),
        out_shape=jax.ShapeDtypeStruct((batch, seq, D_MODEL), BF16),
        compiler_params=_params("parallel", "parallel", "arbitrary"),
        name="sb_attention",
    )(qkv3, qkv3, qkv3, tri)
    return out.reshape(batch * seq, D_MODEL)


def _mlp_kernel(h_ref, o_ref, wo_ref, g_ref, wu_ref, wd_ref, out_ref, xn_ref):
    f = pl.program_id(1)

    def contribution():
        u = jnp.maximum(jnp.dot(xn_ref[...], wu_ref[...], preferred_element_type=F32), 0.0)
        return jnp.dot((u * u).astype(BF16), wd_ref[...], preferred_element_type=F32)

    @pl.when(f == 0)
    def _():
        h1 = h_ref[...] + jnp.dot(o_ref[...], wo_ref[...], preferred_element_type=F32)
        xn_ref[...] = _rms(h1, g_ref[...]).astype(BF16)
        out_ref[...] = h1 + contribution()

    @pl.when(f > 0)
    def _():
        out_ref[...] += contribution()


def _mlp(h, o, w_o, norm_g, w_up, w_down, tm=1024, fc=1024):
    t, d = h.shape
    ff = w_up.shape[1]
    return pl.pallas_call(
        _mlp_kernel,
        grid=(t // tm, ff // fc),
        in_specs=[
            pl.BlockSpec((tm, d), lambda i, f: (i, 0)),
            pl.BlockSpec((tm, o.shape[1]), lambda i, f: (i, 0)),
            pl.BlockSpec(w_o.shape, lambda i, f: (0, 0)),
            pl.BlockSpec((1, d), lambda i, f: (0, 0)),
            pl.BlockSpec((d, fc), lambda i, f: (0, f)),
            pl.BlockSpec((fc, d), lambda i, f: (f, 0)),
        ],
        out_specs=pl.BlockSpec((tm, d), lambda i, f: (i, 0)),
        out_shape=jax.ShapeDtypeStruct((t, d), F32),
        scratch_shapes=[pltpu.VMEM((tm, d), BF16)],
        compiler_params=_params("parallel", "arbitrary"),
        name="mlp",
    )(h, o, w_o, norm_g.reshape(1, d), w_up, w_down)


def _dil_attn_kernel(q_ref, kc_ref, kp_ref, vc_ref, vp_ref, o_ref, lse_ref, *, bias_per_step, max_steps):
    n = pl.program_id(2)
    blk = DIL_BLOCK
    qi = lax.broadcasted_iota(jnp.int32, (blk, 2 * blk), 0)
    kj = lax.broadcasted_iota(jnp.int32, (blk, 2 * blk), 1)
    steps = blk + qi - kj
    in_window = (steps >= 0) & (steps <= max_steps)
    steps_f = steps.astype(F32)
    lane = lax.broadcasted_iota(jnp.int32, (1, LANES), 1)
    lanes = [slice(head * DIL_HEAD_DIM, (head + 1) * DIL_HEAD_DIM) for head in range(DIL_HEADS)]

    def window(prev_ref, cur_ref, sub, sl):
        if sub == 0:
            return jnp.concatenate([prev_ref[:, sl], cur_ref[0:blk, sl]], axis=0)
        return cur_ref[(sub - 1) * blk:(sub + 1) * blk, sl]

    for sub in range(q_ref.shape[0] // blk):
        rows = slice(sub * blk, (sub + 1) * blk)
        valid = in_window & ((kj >= blk) | (n > 0)) if sub == 0 else in_window
        zs = [_dot_t(q_ref[rows, sl], window(kp_ref, kc_ref, sub, sl)) for sl in lanes]
        zs = [jnp.where(valid, z - bias_per_step[head] * steps_f, MASKED) for head, z in enumerate(zs)]
        ms = [jnp.max(z, axis=-1, keepdims=True) for z in zs]
        ps = [jnp.exp(z - m) for z, m in zip(zs, ms)]
        denoms = [jnp.sum(p, axis=-1, keepdims=True) for p in ps]
        pvs = [jnp.dot(p.astype(BF16), window(vp_ref, vc_ref, sub, sl), preferred_element_type=F32)
               for p, sl in zip(ps, lanes)]
        lse_tile = jnp.zeros((blk, LANES), F32)
        for head, sl in enumerate(lanes):
            o_ref[rows, sl] = pvs[head] / denoms[head]
            lse_tile = jnp.where(lane == head, ms[head] + jnp.log(denoms[head]), lse_tile)
        lse_ref[rows, :] = lse_tile


def _dil_attention(qkv, group, batch, seq):
    window, dilation = DIL_GROUPS[group]
    n_groups = len(DIL_GROUPS)
    blk = DIL_BLOCK
    length = seq // dilation
    assert length % blk == 0
    nb = length // blk
    width = DIL_HEADS * DIL_HEAD_DIM
    slopes = [2.0 ** (-8.0 * (group * DIL_HEADS + hd + 1) / (n_groups * DIL_HEADS)) for hd in range(DIL_HEADS)]
    view = qkv.reshape(batch, length, dilation * 3 * width)

    subs = DIL_BLOCKS_PER_STEP
    assert nb % subs == 0
    step_rows = subs * blk

    def col(offset):
        return lambda b, r, n: (b, n, r * 3 + offset)

    def col_prev(offset):
        return lambda b, r, n: (b, jnp.maximum(n * subs - 1, 0), r * 3 + offset)

    o, lse = pl.pallas_call(
        functools.partial(_dil_attn_kernel,
                          bias_per_step=tuple(s * dilation for s in slopes),
                          max_steps=window // dilation),
        grid=(batch, dilation, nb // subs),
        in_specs=[
            pl.BlockSpec((None, step_rows, width), col(0)),
            pl.BlockSpec((None, step_rows, width), col(1)),
            pl.BlockSpec((None, blk, width), col_prev(1)),
            pl.BlockSpec((None, step_rows, width), col(2)),
            pl.BlockSpec((None, blk, width), col_prev(2)),
        ],
        out_specs=[
            pl.BlockSpec((None, step_rows, width), lambda b, r, n: (b, n, r)),
            pl.BlockSpec((None, step_rows, LANES), lambda b, r, n: (b, n, r)),
        ],
        out_shape=[
            jax.ShapeDtypeStruct((batch, length, dilation * width), F32),
            jax.ShapeDtypeStruct((batch, length, dilation * LANES), F32),
        ],
        compiler_params=_params("parallel", "parallel", "arbitrary"),
        name="dil_attention_g%d" % group,
    )(view, view, view, view, view)
    rows = batch * length
    return o.reshape(rows, dilation * width), lse.reshape(rows, dilation * LANES)


def _dil_merge_kernel(o0_ref, o1_ref, o2_ref, l0_ref, l1_ref, l2_ref, mrg_ref, o_tok_ref, l_tok_ref):
    tm = mrg_ref.shape[0]
    slabs = D_MODEL // LANES
    o_in = (o0_ref, o1_ref, o2_ref)
    l_in = (l0_ref, l1_ref, l2_ref)
    for g, (_, dilation) in enumerate(DIL_GROUPS):
        for r in range(dilation):
            rows = slice(None) if dilation == 1 else pl.ds(r, tm // dilation, stride=dilation)
            l_tok_ref[g, rows, :] = l_in[g][:, r * LANES:(r + 1) * LANES]
            for c in range(slabs):
                lo = r * D_MODEL + c * LANES
                o_tok_ref[g * slabs + c, rows, :] = o_in[g][:, lo:lo + LANES]
    lses = [l_tok_ref[g] for g in range(len(DIL_GROUPS))]
    m = jnp.maximum(jnp.maximum(lses[0], lses[1]), lses[2])
    es = [jnp.exp(l - m) for l in lses]
    total = es[0] + es[1] + es[2]
    alphas = [e / total for e in es]
    for head in range(DIL_HEADS):
        merged = sum(alphas[g][:, head:head + 1] * o_tok_ref[g * slabs + head] for g in range(len(DIL_GROUPS)))
        mrg_ref[:, head * DIL_HEAD_DIM:(head + 1) * DIL_HEAD_DIM] = merged.astype(BF16)


def _dil_merge(outs, lses, t, tm=512):
    d = D_MODEL
    n_groups = len(DIL_GROUPS)
    grouped = lambda cols, dil: pl.BlockSpec((tm // dil, dil * cols), lambda i: (i, 0))
    dils = [dil for _, dil in DIL_GROUPS]
    return pl.pallas_call(
        _dil_merge_kernel,
        grid=(t // tm,),
        in_specs=[grouped(d, dil) for dil in dils] + [grouped(LANES, dil) for dil in dils],
        out_specs=pl.BlockSpec((tm, d), lambda i: (i, 0)),
        out_shape=jax.ShapeDtypeStruct((t, d), BF16),
        scratch_shapes=[pltpu.VMEM((n_groups * d // LANES, tm, LANES), F32),
                        pltpu.VMEM((n_groups, tm, LANES), F32)],
        compiler_params=_params("parallel"),
        name="dil_merge",
    )(*outs, *lses)


def _mla_proj_kernel(x_ref, g_ref, win_ref, qa_ref, kva_ref, wq_ref, wk_ref, wv_ref,
                     q_cos_ref, q_up_ref, q_dn_ref, k_cos_ref, k_up_ref, k_dn_ref, ones_ref,
                     q_out, k_out, v_out, qp_ref, kp_ref):
    xn = _rms(x_ref[...], g_ref[...]).astype(BF16)
    c = jnp.dot(xn, win_ref[...], preferred_element_type=F32)
    cq = _rms(c[:, :MLA_Q_RANK], qa_ref[...]).astype(BF16)
    ckv = _rms(c[:, MLA_Q_RANK:MLA_Q_RANK + MLA_KV_RANK], kva_ref[...]).astype(BF16)
    shared_rope = c[:, MLA_Q_RANK + MLA_KV_RANK:]
    qp_ref[...] = jnp.dot(cq, wq_ref[...], preferred_element_type=F32)
    kp_ref[...] = jnp.dot(ckv, wk_ref[...], preferred_element_type=F32)
    v = jnp.dot(ckv, wv_ref[...], preferred_element_type=F32)
    for s in range(v_out.shape[0]):
        v_out[s] = v[s * MLA_KEY_TILE:(s + 1) * MLA_KEY_TILE, :].T.astype(BF16)
    half = MLA_ROPE // 2

    group = ones_ref.shape[0]

    shared = jnp.concatenate([shared_rope] * (group // LANES), axis=1)

    def start(job):
        src_ref, extra, _, _, sl = job
        x = src_ref[:, sl] if extra is None else src_ref[:, sl] + extra
        return x, _split_dot(x * x, ones_ref[...])

    def finish(job, x, sumsq):
        _, _, (cos_ref, up_ref, dn_ref), dst_ref, sl = job
        y = x * lax.rsqrt(sumsq + MLA_QK * EPS)
        rotated = (y * cos_ref[...] + pltpu.roll(y, half, 1) * up_ref[...]
                   + pltpu.roll(y, group - half, 1) * dn_ref[...])
        dst_ref[:, sl] = rotated.astype(BF16)

    jobs = []
    for lo in range(0, MLA_HEADS * LANES, group):
        sl = slice(lo, lo + group)
        jobs.append((qp_ref, None, (q_cos_ref, q_up_ref, q_dn_ref), q_out, sl))
        jobs.append((kp_ref, shared, (k_cos_ref, k_up_ref, k_dn_ref), k_out, sl))
    pending = start(jobs[0])
    for previous, nxt in zip(jobs[:-1], jobs[1:]):
        following = start(nxt)
        finish(previous, *pending)
        pending = following
    finish(jobs[-1], *pending)


def _pad_heads(w, heads, src_lo, src_hi, src_width):
    k = w.shape[0]
    w = w.reshape(k, heads, src_width)[:, :, src_lo:src_hi]
    w = jnp.pad(w, ((0, 0), (0, 0), (0, LANES - (src_hi - src_lo))))
    return w.reshape(k, heads * LANES)


def _mla_proj(h, norm_g, w_in, q_a_gain, kv_a_gain, w_q_b, w_kv_b, q_gain, k_gain, seq, tm=512):
    t, d = h.shape
    latent = MLA_Q_RANK + MLA_KV_RANK
    w_in_p = jnp.concatenate([
        w_in[:, :latent],
        jnp.zeros((d, MLA_NOPE), w_in.dtype),
        w_in[:, latent:],
        jnp.zeros((d, LANES - MLA_QK), w_in.dtype)], axis=1).astype(BF16)
    wq = _pad_heads(w_q_b, MLA_HEADS, 0, MLA_QK, MLA_QK).astype(BF16)
    wk = _pad_heads(w_kv_b, MLA_HEADS, 0, MLA_NOPE, MLA_NOPE + MLA_V).astype(BF16)
    wv = w_kv_b.reshape(MLA_KV_RANK, MLA_HEADS, MLA_NOPE + MLA_V)[:, :, MLA_NOPE:]
    wv = wv.reshape(MLA_KV_RANK, MLA_HEADS * MLA_V).astype(BF16)
    pad = jnp.zeros((LANES - MLA_QK,), F32)
    group = MXU_WIDTH // LANES
    qg = jnp.tile(jnp.concatenate([q_gain, pad]) * LOG2_E, group).reshape(1, MXU_WIDTH)
    kg = jnp.tile(jnp.concatenate([k_gain, pad]) * math.sqrt(MLA_QK), group).reshape(1, MXU_WIDTH)
    half = MLA_ROPE // 2
    inv = ROPE_THETA ** (-jnp.arange(half, dtype=F32) / half)
    ang = jnp.arange(seq, dtype=F32)[:, None] * inv[None, :]
    zeros = jnp.zeros((seq, half), F32)
    cos = jnp.concatenate([jnp.ones((seq, MLA_NOPE), F32), jnp.cos(ang), jnp.cos(ang),
                           jnp.zeros((seq, LANES - MLA_QK), F32)], axis=1)
    sin_up = jnp.concatenate([jnp.zeros((seq, MLA_NOPE), F32), zeros, jnp.sin(ang),
                              jnp.zeros((seq, LANES - MLA_QK), F32)], axis=1)
    sin_dn = jnp.concatenate([jnp.zeros((seq, MLA_NOPE), F32), -jnp.sin(ang), zeros,
                              jnp.zeros((seq, LANES - MLA_QK), F32)], axis=1)
    cos, sin_up, sin_dn = (jnp.tile(table, (1, group)) for table in (cos, sin_up, sin_dn))

    def gained(gain):
        return (cos * gain, sin_up * jnp.roll(gain, half, axis=1), sin_dn * jnp.roll(gain, -half, axis=1))

    tables = gained(qg) + gained(kg)
    lane = jnp.arange(MXU_WIDTH)
    ones = (lane[:, None] // LANES == lane[None, :] // LANES).astype(BF16)
    per_seq = seq // tm
    full = lambda a: pl.BlockSpec(a.shape, lambda i: (0,) * a.ndim)
    table = pl.BlockSpec((tm, MXU_WIDTH), lambda i: (i % per_seq, 0))
    consts = (norm_g.reshape(1, d), w_in_p, q_a_gain.reshape(1, -1), kv_a_gain.reshape(1, -1), wq, wk, wv)
    qk_cols = MLA_HEADS * LANES
    v_rows = MLA_HEADS * MLA_V
    slabs = tm // MLA_KEY_TILE
    return pl.pallas_call(
        _mla_proj_kernel,
        grid=(t // tm,),
        in_specs=[pl.BlockSpec((tm, d), lambda i: (i, 0))] + [full(a) for a in consts]
                 + [table] * len(tables) + [full(ones)],
        out_specs=[
            pl.BlockSpec((tm, qk_cols), lambda i: (i, 0)),
            pl.BlockSpec((tm, qk_cols), lambda i: (i, 0)),
            pl.BlockSpec((slabs, v_rows, MLA_KEY_TILE), lambda i: (i, 0, 0)),
        ],
        out_shape=[
            jax.ShapeDtypeStruct((t, qk_cols), BF16),
            jax.ShapeDtypeStruct((t, qk_cols), BF16),
            jax.ShapeDtypeStruct((t // MLA_KEY_TILE, v_rows, MLA_KEY_TILE), BF16),
        ],
        scratch_shapes=[pltpu.VMEM((tm, qk_cols), F32), pltpu.VMEM((tm, qk_cols), F32)],
        compiler_params=_params("parallel"),
        name="mla_proj",
    )(h, *consts, *tables, ones)


def _mla_attn_kernel(q_ref, k_ref, vt_ref, o_ref):
    i = pl.program_id(2)
    heads = MLA_HEADS_PER_STEP
    sub = MLA_KEY_TILE
    bq = q_ref.shape[0]
    n_sub = bq // sub
    lanes = [slice(head * LANES, (head + 1) * LANES) for head in range(heads)]
    qs = [q_ref[:, sl] for sl in lanes]
    key_idx = lax.broadcasted_iota(jnp.int32, (bq, bq), 0)
    qry_idx = lax.broadcasted_iota(jnp.int32, (bq, bq), 1)

    def tile(j, state, diagonal):
        start = pl.multiple_of(j * bq, bq)
        keys = k_ref[pl.ds(start, bq), :]
        sts = [_dot_t(keys[:, lanes[head]], qs[head]) for head in range(heads)]
        if diagonal:
            sts = [jnp.where(key_idx <= qry_idx, st, MASKED) for st in sts]
        ms = [jnp.maximum(state[head][0], jnp.max(sts[head], axis=0, keepdims=True)) for head in range(heads)]
        pts = [jnp.exp2(sts[head] - ms[head]) for head in range(heads)]
        new_state = []
        for head in range(heads):
            m, l, acc = state[head]
            corr = jnp.exp2(m - ms[head])
            pv = sum(jnp.dot(vt_ref[j * n_sub + s, head * MLA_V:(head + 1) * MLA_V, :],
                             pts[head][s * sub:(s + 1) * sub].astype(BF16), preferred_element_type=F32)
                     for s in range(n_sub))
            new_state.append((ms[head], corr * l + jnp.sum(pts[head], axis=0, keepdims=True), corr * acc + pv))
        return tuple(new_state)

    def fixed_reference_tile(j, state):
        start = pl.multiple_of(j * bq, bq)
        keys = k_ref[pl.ds(start, bq), :]
        sts = [_dot_t(keys[:, lanes[head]], qs[head]) for head in range(heads)]
        pts = [jnp.exp2(sts[head] - state[head][0]) for head in range(heads)]
        new_state = []
        for head in range(heads):
            m, l, acc = state[head]
            pv = sum(jnp.dot(vt_ref[j * n_sub + s, head * MLA_V:(head + 1) * MLA_V, :],
                             pts[head][s * sub:(s + 1) * sub].astype(BF16), preferred_element_type=F32)
                     for s in range(n_sub))
            new_state.append((m, l + jnp.sum(pts[head], axis=0, keepdims=True), acc + pv))
        return tuple(new_state)

    init = tuple((jnp.full((1, bq), MASKED, F32), jnp.zeros((1, bq), F32), jnp.zeros((MLA_V, bq), F32))
                 for _ in range(heads))
    diag = tile(i, init, True)
    fast = lax.fori_loop(0, i, lambda t, s: fixed_reference_tile(t, s), diag)
    largest = functools.reduce(jnp.maximum, [l for _, l, _ in fast])
    in_range = jnp.max(largest) < MLA_MAX_DENOMINATOR

    def finish(state):
        out_t = jnp.concatenate([acc / l for _, l, acc in state], axis=0)
        return out_t.T.astype(BF16)

    o_ref[...] = lax.cond(in_range, lambda: finish(fast),
                          lambda: finish(lax.fori_loop(0, i, lambda t, s: tile(t, s, False), diag)))


def _mla_attention(q, k, vt, batch, seq, bq=512):
    heads = MLA_HEADS_PER_STEP
    sub = MLA_KEY_TILE
    q3 = q.reshape(batch, seq, -1)
    k3 = k.reshape(batch, seq, -1)
    vt4 = vt.reshape(batch, seq // sub, MLA_HEADS * MLA_V, sub)
    out = pl.pallas_call(
        _mla_attn_kernel,
        grid=(batch, MLA_HEADS // heads, seq // bq),
        in_specs=[
            pl.BlockSpec((None, bq, heads * LANES), lambda b, p, i: (b, i, p)),
            pl.BlockSpec((None, seq, heads * LANES), lambda b, p, i: (b, 0, p)),
            pl.BlockSpec((None, seq // sub, heads * MLA_V, sub), lambda b, p, i: (b, 0, p, 0)),
        ],
        out_specs=pl.BlockSpec((None, bq, heads * MLA_V), lambda b, p, i: (b, i, p)),
        out_shape=jax.ShapeDtypeStruct((batch, seq, MLA_HEADS * MLA_V), BF16),
        compiler_params=_params("parallel", "parallel", "arbitrary"),
        name="mla_attention",
    )(q3, k3, vt4)
    return out.reshape(batch * seq, MLA_HEADS * MLA_V)


def _sb_layer(h, batch, seq, mix_norm, w_in, q_norm, k_norm, w_out):
    scale = LOG2_E / math.sqrt(SB_HEAD_DIM)
    gains = jnp.stack([jnp.tile(q_norm, SB_HEADS) * scale, jnp.tile(k_norm, SB_HEADS)])
    qkv = _proj_qkv(h, mix_norm, w_in.astype(BF16), gains, SB_HEAD_DIM)
    return _sb_attention(qkv, batch, seq), w_out


def _dil_layer(h, batch, seq, mix_norm, w_in, q_norm, k_norm, w_out):
    scale = DIL_HEAD_DIM ** -0.5
    w_in = w_in.astype(BF16)
    cols = 3 * D_MODEL
    outs, lses = [], []
    for g, (_, dilation) in enumerate(DIL_GROUPS):
        gains = jnp.stack([jnp.tile(q_norm[g], DIL_HEADS) * scale, jnp.tile(k_norm[g], DIL_HEADS)])
        qkv = _proj_qkv(h, mix_norm, w_in[:, g * cols:(g + 1) * cols], gains, DIL_HEAD_DIM, dilation)
        o, lse = _dil_attention(qkv, g, batch, seq)
        outs.append(o)
        lses.append(lse)
    return _dil_merge(outs, lses, h.shape[0]), w_out


def _mla_layer(h, batch, seq, mix_norm, w_in, q_a_norm, kv_a_norm, w_q_b, w_kv_b, q_norm, k_norm, w_out):
    q, k, v = _mla_proj(h, mix_norm, w_in, q_a_norm, kv_a_norm, w_q_b, w_kv_b, q_norm, k_norm, seq)
    return _mla_attention(q, k, v, batch, seq), w_out


def _trunk(x, layers):
    batch, seq, d = x.shape
    h = x.reshape(batch * seq, d)
    mixers = (_sb_layer, _dil_layer, _mla_layer)
    for idx, (mix_norm, mix_params, mlp_norm, w_up, w_down) in enumerate(layers):
        o, w_out = mixers[idx % len(mixers)](h, batch, seq, mix_norm, *mix_params)
        h = _mlp(h, o, w_out.astype(BF16), mlp_norm, w_up.astype(BF16), w_down.astype(BF16))
    return h.reshape(batch, seq, d)


def kernel(x, l0_mix_norm, l0_sb_w_in, l0_sb_q_norm, l0_sb_k_norm, l0_sb_w_out, l0_mlp_norm, l0_mlp_w_up, l0_mlp_w_down, l1_mix_norm, l1_dil_w_in, l1_dil_q_norm, l1_dil_k_norm, l1_dil_w_out, l1_mlp_norm, l1_mlp_w_up, l1_mlp_w_down, l2_mix_norm, l2_mla_w_in, l2_mla_q_a_norm, l2_mla_kv_a_norm, l2_mla_w_q_b, l2_mla_w_kv_b, l2_mla_q_norm, l2_mla_k_norm, l2_mla_w_out, l2_mlp_norm, l2_mlp_w_up, l2_mlp_w_down, l3_mix_norm, l3_sb_w_in, l3_sb_q_norm, l3_sb_k_norm, l3_sb_w_out, l3_mlp_norm, l3_mlp_w_up, l3_mlp_w_down):
    layers = [
        (l0_mix_norm, (l0_sb_w_in, l0_sb_q_norm, l0_sb_k_norm, l0_sb_w_out),
         l0_mlp_norm, l0_mlp_w_up, l0_mlp_w_down),
        (l1_mix_norm, (l1_dil_w_in, l1_dil_q_norm, l1_dil_k_norm, l1_dil_w_out),
         l1_mlp_norm, l1_mlp_w_up, l1_mlp_w_down),
        (l2_mix_norm, (l2_mla_w_in, l2_mla_q_a_norm, l2_mla_kv_a_norm, l2_mla_w_q_b,
                       l2_mla_w_kv_b, l2_mla_q_norm, l2_mla_k_norm, l2_mla_w_out),
         l2_mlp_norm, l2_mlp_w_up, l2_mlp_w_down),
        (l3_mix_norm, (l3_sb_w_in, l3_sb_q_norm, l3_sb_k_norm, l3_sb_w_out),
         l3_mlp_norm, l3_mlp_w_up, l3_mlp_w_down),
    ]
    return _trunk(x, layers)
```

```python
import functools
import math

import jax
import jax.numpy as jnp
from jax import lax
from jax.experimental import pallas as pl
from jax.experimental.pallas import tpu as pltpu

F32 = jnp.float32
BF16 = jnp.bfloat16

EPS = 1e-6
LANES = 128
MXU_WIDTH = 256
MASKED = -1e30
LOG2_E = math.log2(math.e)
SB_SATURATED = 150.0
SB_LINEAR = 30.0
VMEM_LIMIT_BYTES = 48 * 1024 * 1024

D_MODEL = 1024
SB_HEADS = 16
SB_HEAD_DIM = 64
DIL_GROUPS = ((128, 1), (512, 4), (2048, 16))
DIL_HEADS = 8
DIL_HEAD_DIM = 128
DIL_BLOCK = 128
DIL_BLOCKS_PER_STEP = 2
MLA_HEADS = 16
MLA_NOPE = 64
MLA_ROPE = 32
MLA_V = 64
MLA_QK = MLA_NOPE + MLA_ROPE
MLA_Q_RANK = 384
MLA_KV_RANK = 256
MLA_KEY_TILE = 256
MLA_HEADS_PER_STEP = 4
MLA_MAX_DENOMINATOR = 2.0 ** 64
ROPE_THETA = 10000.0


def _params(*semantics):
    return pltpu.CompilerParams(dimension_semantics=semantics, vmem_limit_bytes=VMEM_LIMIT_BYTES)


def _rms(x, g):
    ms = jnp.mean(x * x, axis=-1, keepdims=True)
    return x * lax.rsqrt(ms + EPS) * g


def _split_dot(x, w):
    hi = x.astype(BF16)
    lo = (x - hi.astype(F32)).astype(BF16)
    return (jnp.dot(hi, w, preferred_element_type=F32)
            + jnp.dot(lo, w, preferred_element_type=F32))


def _dot_t(a, b):
    return lax.dot_general(a, b, (((1,), (1,)), ((), ())), preferred_element_type=F32)


def _proj_qkv_kernel(x_ref, g_ref, w_ref, gain_ref, seg_ref, o_ref, xn_ref, *y_scratch, head_dim, dilation):
    tm, width = xn_ref.shape[0], D_MODEL
    slabs = width // LANES
    seg_width = seg_ref.shape[0]
    xn_ref[...] = _rms(x_ref[...], g_ref[...]).astype(BF16)

    def project(block, lo):
        col = block * width + lo
        return jnp.dot(xn_ref[...], w_ref[:, col:col + seg_width], preferred_element_type=F32)

    def finish(block, lo, yc):
        if block < 2:
            sumsq = _split_dot(yc * yc, seg_ref[...])
            yc = yc * lax.rsqrt(sumsq + head_dim * EPS) * gain_ref[block:block + 1, lo:lo + seg_width]
        for c in range(seg_width // LANES):
            piece = yc[:, c * LANES:(c + 1) * LANES]
            if dilation == 1:
                o_ref[:, block * width + lo + c * LANES:block * width + lo + (c + 1) * LANES] = piece.astype(BF16)
            else:
                y_scratch[0][block * slabs + lo // LANES + c] = piece

    chunks = [(block, lo) for block in range(3) for lo in range(0, width, seg_width)]
    pending = project(*chunks[0])
    for previous, nxt in zip(chunks[:-1], chunks[1:]):
        following = project(*nxt)
        finish(*previous, pending)
        pending = following
    finish(*chunks[-1], pending)
    if dilation > 1:
        for block in range(3):
            for r in range(dilation):
                for c in range(slabs):
                    rows = y_scratch[0][block * slabs + c, pl.ds(r, tm // dilation, stride=dilation), :]
                    lo = (r * 3 + block) * width + c * LANES
                    o_ref[:, lo:lo + LANES] = rows.astype(BF16)


def _proj_qkv(h, norm_g, w, gains, head_dim, dilation=1, tm=512):
    t, d = h.shape
    width = D_MODEL
    assert w.shape == (d, 3 * width) and tm % (16 * dilation) == 0
    lane = jnp.arange(MXU_WIDTH)
    seg = (lane[:, None] // head_dim == lane[None, :] // head_dim).astype(BF16)
    gains = gains * math.sqrt(head_dim)
    return pl.pallas_call(
        functools.partial(_proj_qkv_kernel, head_dim=head_dim, dilation=dilation),
        grid=(t // tm,),
        in_specs=[
            pl.BlockSpec((tm, d), lambda i: (i, 0)),
            pl.BlockSpec((1, d), lambda i: (0, 0)),
            pl.BlockSpec((d, 3 * width), lambda i: (0, 0)),
            pl.BlockSpec((2, width), lambda i: (0, 0)),
            pl.BlockSpec((MXU_WIDTH, MXU_WIDTH), lambda i: (0, 0)),
        ],
        out_specs=pl.BlockSpec((tm // dilation, dilation * 3 * width), lambda i: (i, 0)),
        out_shape=jax.ShapeDtypeStruct((t // dilation, dilation * 3 * width), BF16),
        scratch_shapes=[pltpu.VMEM((tm, d), BF16)]
                       + ([pltpu.VMEM((3 * width // LANES, tm, LANES), F32)] if dilation > 1 else []),
        compiler_params=_params("parallel"),
        name="proj_qkv_d%d" % dilation,
    )(h, norm_g.reshape(1, d), w, gains, seg)


def _sb_attn_kernel(q_ref, k_ref, v_ref, tri_ref, o_ref, *, blk):
    i = pl.program_id(2)
    lane = lax.broadcasted_iota(jnp.int32, (1, LANES), 1)
    row = lax.broadcasted_iota(jnp.int32, (blk, blk), 0)
    col = lax.broadcasted_iota(jnp.int32, (blk, blk), 1)
    strictly_past = col < row
    per_group = LANES // SB_HEAD_DIM
    groups = q_ref.shape[1] // LANES
    heads = groups * per_group
    group_lanes = [slice((head // per_group) * LANES, (head // per_group + 1) * LANES) for head in range(heads)]
    head_lanes = [(lane // SB_HEAD_DIM) == head % per_group for head in range(heads)]
    qs = [jnp.where(head_lanes[head], q_ref[:, group_lanes[head]], jnp.zeros((blk, LANES), BF16))
          for head in range(heads)]

    def tiles(js, masks, state, members=tuple(range(heads))):
        starts = [pl.multiple_of(jnp.maximum(j, 0) * blk, blk) for j in js]
        pairs = [(t, head) for t in range(len(js)) for head in members]
        zs = {}
        for t, head in pairs:
            z = _dot_t(qs[head], k_ref[pl.ds(starts[t], blk), group_lanes[head]])
            zs[t, head] = z if masks[t] is None else jnp.where(masks[t], z, MASKED)
        sps = {p: jnp.where(zs[p] > SB_LINEAR, zs[p], jnp.log2(1.0 + jnp.exp2(zs[p]))) for p in pairs}
        withins = {p: _split_dot(sps[p], tri_ref[...]) for p in pairs}
        later = {head: state[head][1] for head in members}
        probs = {}
        for t, head in pairs:
            probs[t, head] = jnp.exp2(zs[t, head] - withins[t, head] - later[head]).astype(BF16)
            later[head] = later[head] + withins[t, head][:, 0:1]
        acc = {head: state[head][0] for head in members}
        for t, head in pairs:
            vals = v_ref[pl.ds(starts[t], blk), group_lanes[head]]
            acc[head] = acc[head] + jnp.dot(probs[t, head], vals, preferred_element_type=F32)
        return tuple((acc[head], later[head]) if head in members else state[head] for head in range(heads))

    state = tuple((jnp.zeros((blk, LANES), F32), jnp.zeros((blk, 1), F32)) for _ in range(heads))
    state = tiles([i, i - 1], [strictly_past, i > 0], state)
    remaining = jnp.maximum(i - 1, 0)

    def unsaturated(s, members):
        lowest = functools.reduce(jnp.minimum, [s[head][1] for head in members])
        return (jnp.min(lowest) < SB_SATURATED).astype(jnp.int32)

    for group in range(groups):
        members = tuple(range(group * per_group, (group + 1) * per_group))

        def one_more(carry, members=members):
            t, _, s = carry
            s = tiles([i - 2 - t], [None], s, members)
            return t + 1, unsaturated(s, members), s

        _, _, state = lax.while_loop(lambda c: jnp.logical_and(c[0] < remaining, c[1] > 0), one_more,
                                     (jnp.int32(0), unsaturated(state, members), state))
    for group in range(groups):
        members = range(group * per_group, (group + 1) * per_group)
        out = sum(jnp.where(head_lanes[head], state[head][0], 0.0) for head in members)
        o_ref[:, group * LANES:(group + 1) * LANES] = out.astype(BF16)


def _sb_attention(qkv, batch, seq, blk=256, groups=2):
    width = groups * LANES
    steps = D_MODEL // width
    nq = seq // blk
    qkv3 = qkv.reshape(batch, seq, 3 * D_MODEL)
    idx = jnp.arange(blk)
    tri = (idx[:, None] >= idx[None, :]).astype(BF16)
    out = pl.pallas_call(
        functools.partial(_sb_attn_kernel, blk=blk),
        grid=(batch, steps, nq),
        in_specs=[
            pl.BlockSpec((None, blk, width), lambda b, p, i: (b, i, p)),
            pl.BlockSpec((None, seq, width), lambda b, p, i: (b, 0, steps + p)),
            pl.BlockSpec((None, seq, width), lambda b, p, i: (b, 0, 2 * steps + p)),
            pl.BlockSpec((blk, blk), lambda b, p, i: (0, 0)),
        ],
        out_specs=pl.BlockSpec((None, blk, width), lambda b, p, i: (b, i, p)),
        out_shape=jax.ShapeDtypeStruct((batch, seq, D_MODEL), BF16),
        compiler_params=_params("parallel", "parallel", "arbitrary"),
        name="sb_attention",
    )(qkv3, qkv3, qkv3, tri)
    return out.reshape(batch * seq, D_MODEL)


def _mlp_kernel(h_ref, o_ref, wo_ref, g_ref, wu_ref, wd_ref, out_ref, xn_ref):
    f = pl.program_id(1)

    def contribution():
        u = jnp.maximum(jnp.dot(xn_ref[...], wu_ref[...], preferred_element_type=F32), 0.0)
        return jnp.dot((u * u).astype(BF16), wd_ref[...], preferred_element_type=F32)

    @pl.when(f == 0)
    def _():
        h1 = h_ref[...] + jnp.dot(o_ref[...], wo_ref[...], preferred_element_type=F32)
        xn_ref[...] = _rms(h1, g_ref[...]).astype(BF16)
        out_ref[...] = h1 + contribution()

    @pl.when(f > 0)
    def _():
        out_ref[...] += contribution()


def _mlp(h, o, w_o, norm_g, w_up, w_down, tm=1024, fc=1024):
    t, d = h.shape
    ff = w_up.shape[1]
    return pl.pallas_call(
        _mlp_kernel,
        grid=(t // tm, ff // fc),
        in_specs=[
            pl.BlockSpec((tm, d), lambda i, f: (i, 0)),
            pl.BlockSpec((tm, o.shape[1]), lambda i, f: (i, 0)),
            pl.BlockSpec(w_o.shape, lambda i, f: (0, 0)),
            pl.BlockSpec((1, d), lambda i, f: (0, 0)),
            pl.BlockSpec((d, fc), lambda i, f: (0, f)),
            pl.BlockSpec((fc, d), lambda i, f: (f, 0)),
        ],
        out_specs=pl.BlockSpec((tm, d), lambda i, f: (i, 0)),
        out_shape=jax.ShapeDtypeStruct((t, d), F32),
        scratch_shapes=[pltpu.VMEM((tm, d), BF16)],
        compiler_params=_params("parallel", "arbitrary"),
        name="mlp",
    )(h, o, w_o, norm_g.reshape(1, d), w_up, w_down)


def _dil_attn_kernel(q_ref, kc_ref, kp_ref, vc_ref, vp_ref, o_ref, lse_ref, *, bias_per_step, max_steps):
    n = pl.program_id(2)
    blk = DIL_BLOCK
    qi = lax.broadcasted_iota(jnp.int32, (blk, 2 * blk), 0)
    kj = lax.broadcasted_iota(jnp.int32, (blk, 2 * blk), 1)
    steps = blk + qi - kj
    in_window = (steps >= 0) & (steps <= max_steps)
    steps_f = steps.astype(F32)
    lane = lax.broadcasted_iota(jnp.int32, (1, LANES), 1)
    lanes = [slice(head * DIL_HEAD_DIM, (head + 1) * DIL_HEAD_DIM) for head in range(DIL_HEADS)]

    def window(prev_ref, cur_ref, sub, sl):
        if sub == 0:
            return jnp.concatenate([prev_ref[:, sl], cur_ref[0:blk, sl]], axis=0)
        return cur_ref[(sub - 1) * blk:(sub + 1) * blk, sl]

    for sub in range(q_ref.shape[0] // blk):
        rows = slice(sub * blk, (sub + 1) * blk)
        valid = in_window & ((kj >= blk) | (n > 0)) if sub == 0 else in_window
        zs = [_dot_t(q_ref[rows, sl], window(kp_ref, kc_ref, sub, sl)) for sl in lanes]
        zs = [jnp.where(valid, z - bias_per_step[head] * steps_f, MASKED) for head, z in enumerate(zs)]
        ms = [jnp.max(z, axis=-1, keepdims=True) for z in zs]
        ps = [jnp.exp(z - m) for z, m in zip(zs, ms)]
        denoms = [jnp.sum(p, axis=-1, keepdims=True) for p in ps]
        pvs = [jnp.dot(p.astype(BF16), window(vp_ref, vc_ref, sub, sl), preferred_element_type=F32)
               for p, sl in zip(ps, lanes)]
        lse_tile = jnp.zeros((blk, LANES), F32)
        for head, sl in enumerate(lanes):
            o_ref[rows, sl] = pvs[head] / denoms[head]
            lse_tile = jnp.where(lane == head, ms[head] + jnp.log(denoms[head]), lse_tile)
        lse_ref[rows, :] = lse_tile


def _dil_attention(qkv, group, batch, seq):
    window, dilation = DIL_GROUPS[group]
    n_groups = len(DIL_GROUPS)
    blk = DIL_BLOCK
    length = seq // dilation
    assert length % blk == 0
    nb = length // blk
    width = DIL_HEADS * DIL_HEAD_DIM
    slopes = [2.0 ** (-8.0 * (group * DIL_HEADS + hd + 1) / (n_groups * DIL_HEADS)) for hd in range(DIL_HEADS)]
    view = qkv.reshape(batch, length, dilation * 3 * width)

    subs = DIL_BLOCKS_PER_STEP
    assert nb % subs == 0
    step_rows = subs * blk

    def col(offset):
        return lambda b, r, n: (b, n, r * 3 + offset)

    def col_prev(offset):
        return lambda b, r, n: (b, jnp.maximum(n * subs - 1, 0), r * 3 + offset)

    o, lse = pl.pallas_call(
        functools.partial(_dil_attn_kernel,
                          bias_per_step=tuple(s * dilation for s in slopes),
                          max_steps=window // dilation),
        grid=(batch, dilation, nb // subs),
        in_specs=[
            pl.BlockSpec((None, step_rows, width), col(0)),
            pl.BlockSpec((None, step_rows, width), col(1)),
            pl.BlockSpec((None, blk, width), col_prev(1)),
            pl.BlockSpec((None, step_rows, width), col(2)),
            pl.BlockSpec((None, blk, width), col_prev(2)),
        ],
        out_specs=[
            pl.BlockSpec((None, step_rows, width), lambda b, r, n: (b, n, r)),
            pl.BlockSpec((None, step_rows, LANES), lambda b, r, n: (b, n, r)),
        ],
        out_shape=[
            jax.ShapeDtypeStruct((batch, length, dilation * width), F32),
            jax.ShapeDtypeStruct((batch, length, dilation * LANES), F32),
        ],
        compiler_params=_params("parallel", "parallel", "arbitrary"),
        name="dil_attention_g%d" % group,
    )(view, view, view, view, view)
    rows = batch * length
    return o.reshape(rows, dilation * width), lse.reshape(rows, dilation * LANES)


def _dil_merge_kernel(o0_ref, o1_ref, o2_ref, l0_ref, l1_ref, l2_ref, mrg_ref, o_tok_ref, l_tok_ref):
    tm = mrg_ref.shape[0]
    slabs = D_MODEL // LANES
    o_in = (o0_ref, o1_ref, o2_ref)
    l_in = (l0_ref, l1_ref, l2_ref)
    for g, (_, dilation) in enumerate(DIL_GROUPS):
        for r in range(dilation):
            rows = slice(None) if dilation == 1 else pl.ds(r, tm // dilation, stride=dilation)
            l_tok_ref[g, rows, :] = l_in[g][:, r * LANES:(r + 1) * LANES]
            for c in range(slabs):
                lo = r * D_MODEL + c * LANES
                o_tok_ref[g * slabs + c, rows, :] = o_in[g][:, lo:lo + LANES]
    lses = [l_tok_ref[g] for g in range(len(DIL_GROUPS))]
    m = jnp.maximum(jnp.maximum(lses[0], lses[1]), lses[2])
    es = [jnp.exp(l - m) for l in lses]
    total = es[0] + es[1] + es[2]
    alphas = [e / total for e in es]
    for head in range(DIL_HEADS):
        merged = sum(alphas[g][:, head:head + 1] * o_tok_ref[g * slabs + head] for g in range(len(DIL_GROUPS)))
        mrg_ref[:, head * DIL_HEAD_DIM:(head + 1) * DIL_HEAD_DIM] = merged.astype(BF16)


def _dil_merge(outs, lses, t, tm=512):
    d = D_MODEL
    n_groups = len(DIL_GROUPS)
    grouped = lambda cols, dil: pl.BlockSpec((tm // dil, dil * cols), lambda i: (i, 0))
    dils = [dil for _, dil in DIL_GROUPS]
    return pl.pallas_call(
        _dil_merge_kernel,
        grid=(t // tm,),
        in_specs=[grouped(d, dil) for dil in dils] + [grouped(LANES, dil) for dil in dils],
        out_specs=pl.BlockSpec((tm, d), lambda i: (i, 0)),
        out_shape=jax.ShapeDtypeStruct((t, d), BF16),
        scratch_shapes=[pltpu.VMEM((n_groups * d // LANES, tm, LANES), F32),
                        pltpu.VMEM((n_groups, tm, LANES), F32)],
        compiler_params=_params("parallel"),
        name="dil_merge",
    )(*outs, *lses)


def _mla_proj_kernel(x_ref, g_ref, win_ref, qa_ref, kva_ref, wq_ref, wk_ref, wv_ref, qg_ref, kg_ref,
                     cos_ref, sin_up_ref, sin_dn_ref, ones_ref, q_out, k_out, v_out, qp_ref, kp_ref):
    xn = _rms(x_ref[...], g_ref[...]).astype(BF16)
    c = jnp.dot(xn, win_ref[...], preferred_element_type=F32)
    cq = _rms(c[:, :MLA_Q_RANK], qa_ref[...]).astype(BF16)
    ckv = _rms(c[:, MLA_Q_RANK:MLA_Q_RANK + MLA_KV_RANK], kva_ref[...]).astype(BF16)
    shared_rope = c[:, MLA_Q_RANK + MLA_KV_RANK:]
    qp_ref[...] = jnp.dot(cq, wq_ref[...], preferred_element_type=F32)
    kp_ref[...] = jnp.dot(ckv, wk_ref[...], preferred_element_type=F32)
    v = jnp.dot(ckv, wv_ref[...], preferred_element_type=F32)
    for s in range(v_out.shape[0]):
        v_out[s] = v[s * MLA_KEY_TILE:(s + 1) * MLA_KEY_TILE, :].T.astype(BF16)
    cos = cos_ref[...]
    sin_up = sin_up_ref[...]
    sin_dn = sin_dn_ref[...]
    half = MLA_ROPE // 2

    group = ones_ref.shape[0]

    def norm_rope(x, gain):
        y = x * lax.rsqrt(_split_dot(x * x, ones_ref[...]) + MLA_QK * EPS) * gain
        return y * cos + pltpu.roll(y, half, 1) * sin_up + pltpu.roll(y, group - half, 1) * sin_dn

    shared = jnp.concatenate([shared_rope] * (group // LANES), axis=1)
    for lo in range(0, MLA_HEADS * LANES, group):
        sl = slice(lo, lo + group)
        q_out[:, sl] = norm_rope(qp_ref[:, sl], qg_ref[...]).astype(BF16)
        k_out[:, sl] = norm_rope(kp_ref[:, sl] + shared, kg_ref[...]).astype(BF16)


def _pad_heads(w, heads, src_lo, src_hi, src_width):
    k = w.shape[0]
    w = w.reshape(k, heads, src_width)[:, :, src_lo:src_hi]
    w = jnp.pad(w, ((0, 0), (0, 0), (0, LANES - (src_hi - src_lo))))
    return w.reshape(k, heads * LANES)


def _mla_proj(h, norm_g, w_in, q_a_gain, kv_a_gain, w_q_b, w_kv_b, q_gain, k_gain, seq, tm=512):
    t, d = h.shape
    latent = MLA_Q_RANK + MLA_KV_RANK
    w_in_p = jnp.concatenate([
        w_in[:, :latent],
        jnp.zeros((d, MLA_NOPE), w_in.dtype),
        w_in[:, latent:],
        jnp.zeros((d, LANES - MLA_QK), w_in.dtype)], axis=1).astype(BF16)
    wq = _pad_heads(w_q_b, MLA_HEADS, 0, MLA_QK, MLA_QK).astype(BF16)
    wk = _pad_heads(w_kv_b, MLA_HEADS, 0, MLA_NOPE, MLA_NOPE + MLA_V).astype(BF16)
    wv = w_kv_b.reshape(MLA_KV_RANK, MLA_HEADS, MLA_NOPE + MLA_V)[:, :, MLA_NOPE:]
    wv = wv.reshape(MLA_KV_RANK, MLA_HEADS * MLA_V).astype(BF16)
    pad = jnp.zeros((LANES - MLA_QK,), F32)
    group = MXU_WIDTH // LANES
    qg = jnp.tile(jnp.concatenate([q_gain, pad]) * LOG2_E, group).reshape(1, MXU_WIDTH)
    kg = jnp.tile(jnp.concatenate([k_gain, pad]) * math.sqrt(MLA_QK), group).reshape(1, MXU_WIDTH)
    half = MLA_ROPE // 2
    inv = ROPE_THETA ** (-jnp.arange(half, dtype=F32) / half)
    ang = jnp.arange(seq, dtype=F32)[:, None] * inv[None, :]
    zeros = jnp.zeros((seq, half), F32)
    cos = jnp.concatenate([jnp.ones((seq, MLA_NOPE), F32), jnp.cos(ang), jnp.cos(ang),
                           jnp.zeros((seq, LANES - MLA_QK), F32)], axis=1)
    sin_up = jnp.concatenate([jnp.zeros((seq, MLA_NOPE), F32), zeros, jnp.sin(ang),
                              jnp.zeros((seq, LANES - MLA_QK), F32)], axis=1)
    sin_dn = jnp.concatenate([jnp.zeros((seq, MLA_NOPE), F32), -jnp.sin(ang), zeros,
                              jnp.zeros((seq, LANES - MLA_QK), F32)], axis=1)
    cos, sin_up, sin_dn = (jnp.tile(table, (1, group)) for table in (cos, sin_up, sin_dn))
    lane = jnp.arange(MXU_WIDTH)
    ones = (lane[:, None] // LANES == lane[None, :] // LANES).astype(BF16)
    per_seq = seq // tm
    full = lambda a: pl.BlockSpec(a.shape, lambda i: (0,) * a.ndim)
    table = pl.BlockSpec((tm, MXU_WIDTH), lambda i: (i % per_seq, 0))
    consts = (norm_g.reshape(1, d), w_in_p, q_a_gain.reshape(1, -1), kv_a_gain.reshape(1, -1), wq, wk, wv, qg, kg)
    qk_cols = MLA_HEADS * LANES
    v_rows = MLA_HEADS * MLA_V
    slabs = tm // MLA_KEY_TILE
    return pl.pallas_call(
        _mla_proj_kernel,
        grid=(t // tm,),
        in_specs=[pl.BlockSpec((tm, d), lambda i: (i, 0))] + [full(a) for a in consts]
                 + [table, table, table, full(ones)],
        out_specs=[
            pl.BlockSpec((tm, qk_cols), lambda i: (i, 0)),
            pl.BlockSpec((tm, qk_cols), lambda i: (i, 0)),
            pl.BlockSpec((slabs, v_rows, MLA_KEY_TILE), lambda i: (i, 0, 0)),
        ],
        out_shape=[
            jax.ShapeDtypeStruct((t, qk_cols), BF16),
            jax.ShapeDtypeStruct((t, qk_cols), BF16),
            jax.ShapeDtypeStruct((t // MLA_KEY_TILE, v_rows, MLA_KEY_TILE), BF16),
        ],
        scratch_shapes=[pltpu.VMEM((tm, qk_cols), F32), pltpu.VMEM((tm, qk_cols), F32)],
        compiler_params=_params("parallel"),
        name="mla_proj",
    )(h, *consts, cos, sin_up, sin_dn, ones)


def _mla_attn_kernel(q_ref, k_ref, vt_ref, o_ref):
    i = pl.program_id(2)
    heads = MLA_HEADS_PER_STEP
    sub = MLA_KEY_TILE
    bq = q_ref.shape[0]
    n_sub = bq // sub
    lanes = [slice(head * LANES, (head + 1) * LANES) for head in range(heads)]
    qs = [q_ref[:, sl] for sl in lanes]
    key_idx = lax.broadcasted_iota(jnp.int32, (bq, bq), 0)
    qry_idx = lax.broadcasted_iota(jnp.int32, (bq, bq), 1)

    def tile(j, state, diagonal):
        start = pl.multiple_of(j * bq, bq)
        keys = k_ref[pl.ds(start, bq), :]
        sts = [_dot_t(keys[:, lanes[head]], qs[head]) for head in range(heads)]
        if diagonal:
            sts = [jnp.where(key_idx <= qry_idx, st, MASKED) for st in sts]
        ms = [jnp.maximum(state[head][0], jnp.max(sts[head], axis=0, keepdims=True)) for head in range(heads)]
        pts = [jnp.exp2(sts[head] - ms[head]) for head in range(heads)]
        new_state = []
        for head in range(heads):
            m, l, acc = state[head]
            corr = jnp.exp2(m - ms[head])
            pv = sum(jnp.dot(vt_ref[j * n_sub + s, head * MLA_V:(head + 1) * MLA_V, :],
                             pts[head][s * sub:(s + 1) * sub].astype(BF16), preferred_element_type=F32)
                     for s in range(n_sub))
            new_state.append((ms[head], corr * l + jnp.sum(pts[head], axis=0, keepdims=True), corr * acc + pv))
        return tuple(new_state)

    def fixed_reference_tile(j, state):
        start = pl.multiple_of(j * bq, bq)
        keys = k_ref[pl.ds(start, bq), :]
        sts = [_dot_t(keys[:, lanes[head]], qs[head]) for head in range(heads)]
        pts = [jnp.exp2(sts[head] - state[head][0]) for head in range(heads)]
        new_state = []
        for head in range(heads):
            m, l, acc = state[head]
            pv = sum(jnp.dot(vt_ref[j * n_sub + s, head * MLA_V:(head + 1) * MLA_V, :],
                             pts[head][s * sub:(s + 1) * sub].astype(BF16), preferred_element_type=F32)
                     for s in range(n_sub))
            new_state.append((m, l + jnp.sum(pts[head], axis=0, keepdims=True), acc + pv))
        return tuple(new_state)

    init = tuple((jnp.full((1, bq), MASKED, F32), jnp.zeros((1, bq), F32), jnp.zeros((MLA_V, bq), F32))
                 for _ in range(heads))
    diag = tile(i, init, True)
    fast = lax.fori_loop(0, i, lambda t, s: fixed_reference_tile(t, s), diag)
    largest = functools.reduce(jnp.maximum, [l for _, l, _ in fast])
    in_range = jnp.max(largest) < MLA_MAX_DENOMINATOR

    def finish(state):
        out_t = jnp.concatenate([acc / l for _, l, acc in state], axis=0)
        return out_t.T.astype(BF16)

    o_ref[...] = lax.cond(in_range, lambda: finish(fast),
                          lambda: finish(lax.fori_loop(0, i, lambda t, s: tile(t, s, False), diag)))


def _mla_attention(q, k, vt, batch, seq, bq=512):
    heads = MLA_HEADS_PER_STEP
    sub = MLA_KEY_TILE
    q3 = q.reshape(batch, seq, -1)
    k3 = k.reshape(batch, seq, -1)
    vt4 = vt.reshape(batch, seq // sub, MLA_HEADS * MLA_V, sub)
    out = pl.pallas_call(
        _mla_attn_kernel,
        grid=(batch, MLA_HEADS // heads, seq // bq),
        in_specs=[
            pl.BlockSpec((None, bq, heads * LANES), lambda b, p, i: (b, i, p)),
            pl.BlockSpec((None, seq, heads * LANES), lambda b, p, i: (b, 0, p)),
            pl.BlockSpec((None, seq // sub, heads * MLA_V, sub), lambda b, p, i: (b, 0, p, 0)),
        ],
        out_specs=pl.BlockSpec((None, bq, heads * MLA_V), lambda b, p, i: (b, i, p)),
        out_shape=jax.ShapeDtypeStruct((batch, seq, MLA_HEADS * MLA_V), BF16),
        compiler_params=_params("parallel", "parallel", "arbitrary"),
        name="mla_attention",
    )(q3, k3, vt4)
    return out.reshape(batch * seq, MLA_HEADS * MLA_V)


def _sb_layer(h, batch, seq, mix_norm, w_in, q_norm, k_norm, w_out):
    scale = LOG2_E / math.sqrt(SB_HEAD_DIM)
    gains = jnp.stack([jnp.tile(q_norm, SB_HEADS) * scale, jnp.tile(k_norm, SB_HEADS)])
    qkv = _proj_qkv(h, mix_norm, w_in.astype(BF16), gains, SB_HEAD_DIM)
    return _sb_attention(qkv, batch, seq), w_out


def _dil_layer(h, batch, seq, mix_norm, w_in, q_norm, k_norm, w_out):
    scale = DIL_HEAD_DIM ** -0.5
    w_in = w_in.astype(BF16)
    cols = 3 * D_MODEL
    outs, lses = [], []
    for g, (_, dilation) in enumerate(DIL_GROUPS):
        gains = jnp.stack([jnp.tile(q_norm[g], DIL_HEADS) * scale, jnp.tile(k_norm[g], DIL_HEADS)])
        qkv = _proj_qkv(h, mix_norm, w_in[:, g * cols:(g + 1) * cols], gains, DIL_HEAD_DIM, dilation)
        o, lse = _dil_attention(qkv, g, batch, seq)
        outs.append(o)
        lses.append(lse)
    return _dil_merge(outs, lses, h.shape[0]), w_out


def _mla_layer(h, batch, seq, mix_norm, w_in, q_a_norm, kv_a_norm, w_q_b, w_kv_b, q_norm, k_norm, w_out):
    q, k, v = _mla_proj(h, mix_norm, w_in, q_a_norm, kv_a_norm, w_q_b, w_kv_b, q_norm, k_norm, seq)
    return _mla_attention(q, k, v, batch, seq), w_out


def _trunk(x, layers):
    batch, seq, d = x.shape
    h = x.reshape(batch * seq, d)
    mixers = (_sb_layer, _dil_layer, _mla_layer)
    for idx, (mix_norm, mix_params, mlp_norm, w_up, w_down) in enumerate(layers):
        o, w_out = mixers[idx % len(mixers)](h, batch, seq, mix_norm, *mix_params)
        h = _mlp(h, o, w_out.astype(BF16), mlp_norm, w_up.astype(BF16), w_down.astype(BF16))
    return h.reshape(batch, seq, d)


def kernel(x, l0_mix_norm, l0_sb_w_in, l0_sb_q_norm, l0_sb_k_norm, l0_sb_w_out, l0_mlp_norm, l0_mlp_w_up, l0_mlp_w_down, l1_mix_norm, l1_dil_w_in, l1_dil_q_norm, l1_dil_k_norm, l1_dil_w_out, l1_mlp_norm, l1_mlp_w_up, l1_mlp_w_down, l2_mix_norm, l2_mla_w_in, l2_mla_q_a_norm, l2_mla_kv_a_norm, l2_mla_w_q_b, l2_mla_w_kv_b, l2_mla_q_norm, l2_mla_k_norm, l2_mla_w_out, l2_mlp_norm, l2_mlp_w_up, l2_mlp_w_down, l3_mix_norm, l3_sb_w_in, l3_sb_q_norm, l3_sb_k_norm, l3_sb_w_out, l3_mlp_norm, l3_mlp_w_up, l3_mlp_w_down):
    layers = [
        (l0_mix_norm, (l0_sb_w_in, l0_sb_q_norm, l0_sb_k_norm, l0_sb_w_out),
         l0_mlp_norm, l0_mlp_w_up, l0_mlp_w_down),
        (l1_mix_norm, (l1_dil_w_in, l1_dil_q_norm, l1_dil_k_norm, l1_dil_w_out),
         l1_mlp_norm, l1_mlp_w_up, l1_mlp_w_down),
        (l2_mix_norm, (l2_mla_w_in, l2_mla_q_a_norm, l2_mla_kv_a_norm, l2_mla_w_q_b,
                       l2_mla_w_kv_b, l2_mla_q_norm, l2_mla_k_norm, l2_mla_w_out),
         l2_mlp_norm, l2_mlp_w_up, l2_mlp_w_down),
        (l3_mix_norm, (l3_sb_w_in, l3_sb_q_norm, l3_sb_k_norm, l3_sb_w_out),
         l3_mlp_norm, l3_mlp_w_up, l3_mlp_w_down),
    ]
    return _trunk(x, layers)
```
